```python
import jax, jax.numpy as jnp
from jax import lax
import numpy as np

D_MODEL = 1024
BATCH = 4
SEQ = 8192
DEPTH = 1

N_Q_HEADS = 16
N_KV_HEADS = 2
HEAD_DIM = 64
Q_PER_KV = N_Q_HEADS // N_KV_HEADS
WINDOW = 128
BLOCK = 128
ATTN_WIDTH = N_Q_HEADS * HEAD_DIM
KV_WIDTH = N_KV_HEADS * HEAD_DIM
POOL_WINDOWS = (2, 4, 8, 16)
N_POOL_GROUPS = len(POOL_WINDOWS)
POOL_WIDTH = 512
POOL_GROUP = POOL_WIDTH // N_POOL_GROUPS
D_FF = 2816
NORM_EPS = 1e-6
IN_SPLITS = tuple(int(s) for s in np.cumsum([ATTN_WIDTH, KV_WIDTH, KV_WIDTH, POOL_WIDTH, D_MODEL]))
IN_WIDTH = ATTN_WIDTH + 2 * KV_WIDTH + POOL_WIDTH + 2 * D_MODEL

kernel_name = "hybrid_swa_sink_alibi_pool_macaron"


def alibi_slopes():
    h = np.arange(1, N_Q_HEADS + 1, dtype=np.float32)
    return jnp.asarray(2.0 ** (-8.0 * h / N_Q_HEADS), dtype=jnp.float32).reshape(N_KV_HEADS, Q_PER_KV)


def rmsnorm(x, g):
    xf = x.astype(jnp.float32)
    y = xf * lax.rsqrt(jnp.mean(xf * xf, axis=-1, keepdims=True) + NORM_EPS)
    return (y * g.astype(jnp.float32)).astype(x.dtype)


def swiglu(x, w_up, w_down):
    a, b = jnp.split(x @ w_up, 2, axis=-1)
    return (jax.nn.silu(a) * b) @ w_down


def sliding_window_attention(q, k, v, sinks):
    B, S, _ = q.shape
    nb = S // BLOCK
    q = q.reshape(B, nb, BLOCK, N_KV_HEADS, Q_PER_KV, HEAD_DIM)
    k = k.reshape(B, S, N_KV_HEADS, HEAD_DIM)
    v = v.reshape(B, S, N_KV_HEADS, HEAD_DIM)
    pad = jnp.zeros((B, BLOCK, N_KV_HEADS, HEAD_DIM), k.dtype)

    def band(t):
        cur = t.reshape(B, nb, BLOCK, N_KV_HEADS, HEAD_DIM)
        prev = jnp.concatenate([pad, t[:, :S - BLOCK]], axis=1).reshape(B, nb, BLOCK, N_KV_HEADS, HEAD_DIM)
        return jnp.concatenate([prev, cur], axis=2)

    kb, vb = band(k), band(v)
    scale = HEAD_DIM ** -0.5
    scores = jnp.einsum('bnqhgd,bnkhd->bnhgqk', q, kb, preferred_element_type=jnp.float32) * scale
    qi = jnp.arange(BLOCK)[:, None] + BLOCK
    kj = jnp.arange(2 * BLOCK)[None, :]
    dist = (qi - kj)
    blk = jnp.arange(nb)[:, None, None]
    valid = (dist >= 0)[None] & (dist < WINDOW)[None] & (blk * BLOCK - BLOCK + kj[None] >= 0)
    slopes = alibi_slopes()[:, :, None, None]
    scores = scores - slopes * dist.astype(jnp.float32)
    scores = jnp.where(valid[None, :, None, None], scores, -jnp.inf)
    sink = sinks.astype(jnp.float32).reshape(N_KV_HEADS, Q_PER_KV)[:, :, None, None]
    m = jnp.maximum(jnp.max(scores, axis=-1, keepdims=True), sink)
    p = jnp.exp(scores - m)
    probs = p / (jnp.sum(p, axis=-1, keepdims=True) + jnp.exp(sink - m))
    out = jnp.einsum('bnhgqk,bnkhd->bnqhgd', probs.astype(vb.dtype), vb)
    return out.reshape(B, S, ATTN_WIDTH)


def multiscale_pool(z, w_mix, scale):
    B, S, _ = z.shape
    zf = z.astype(jnp.float32)
    c = jnp.concatenate([jnp.zeros((B, 1, POOL_WIDTH), jnp.float32), jnp.cumsum(zf, axis=1)], axis=1)
    t = jnp.arange(S)
    outs = []
    for gi, w in enumerate(POOL_WINDOWS):
        cg = c[:, :, gi * POOL_GROUP:(gi + 1) * POOL_GROUP]
        prev = jnp.concatenate([jnp.zeros((B, w - 1, POOL_GROUP), jnp.float32), cg[:, :S - w + 1]], axis=1)
        cnt = jnp.minimum(t + 1, w).astype(jnp.float32)[None, :, None]
        outs.append((cg[:, 1:] - prev) / cnt)
    pooled = (jnp.concatenate(outs, axis=-1) - zf).astype(z.dtype)
    pooled = pooled.reshape(B, S, N_POOL_GROUPS, POOL_GROUP)
    mixed = jnp.einsum('bsgc,gcd->bsgd', pooled, w_mix).reshape(B, S, POOL_WIDTH)
    return mixed * scale


def setup_inputs(seed: int = 0) -> dict:
    key = jax.random.key(seed)
    ks = jax.random.split(key, 20)
    f32 = jnp.float32

    def nrm(k, shape, fan_in):
        return jax.random.normal(k, shape, f32) * (fan_in ** -0.5)

    def gain(k, shape):
        return 1.0 + 0.02 * jax.random.normal(k, shape, f32)

    L = DEPTH
    return {
        "x": jax.random.normal(ks[0], (BATCH, SEQ, D_MODEL), f32),
        "ffn1_norm": gain(ks[1], (L, D_MODEL)),
        "ffn1_w_up": nrm(ks[2], (L, D_MODEL, 2 * D_FF), D_MODEL),
        "ffn1_w_down": nrm(ks[3], (L, D_FF, D_MODEL), D_FF),
        "mix_norm": gain(ks[4], (L, D_MODEL)),
        "w_in": nrm(ks[5], (L, D_MODEL, IN_WIDTH), D_MODEL),
        "sinks": jax.random.normal(ks[6], (L, N_Q_HEADS), f32),
        "w_attn_up": nrm(ks[7], (L, ATTN_WIDTH, D_MODEL), ATTN_WIDTH),
        "pool_w_mix": nrm(ks[8], (L, N_POOL_GROUPS, POOL_GROUP, POOL_GROUP), POOL_GROUP),
        "pool_scale": gain(ks[9], (L, POOL_WIDTH)),
        "w_pool_up": nrm(ks[10], (L, POOL_WIDTH, D_MODEL), POOL_WIDTH),
        "w_out": nrm(ks[11], (L, D_MODEL, D_MODEL), D_MODEL),
        "ffn2_norm": gain(ks[12], (L, D_MODEL)),
        "ffn2_w_up": nrm(ks[13], (L, D_MODEL, 2 * D_FF), D_MODEL),
        "ffn2_w_down": nrm(ks[14], (L, D_FF, D_MODEL), D_FF),
        "final_norm": gain(ks[15], (D_MODEL,)),
    }


def reference(x, ffn1_norm, ffn1_w_up, ffn1_w_down, mix_norm, w_in, sinks, w_attn_up,
              pool_w_mix, pool_scale, w_pool_up, w_out, ffn2_norm, ffn2_w_up, ffn2_w_down,
              final_norm):
    h = x
    for l in range(DEPTH):
        h = h + 0.5 * swiglu(rmsnorm(h, ffn1_norm[l]), ffn1_w_up[l], ffn1_w_down[l])
        u = rmsnorm(h, mix_norm[l])
        q, k, v, z, g_attn, g_pool = jnp.split(u @ w_in[l], IN_SPLITS, axis=-1)
        a = sliding_window_attention(q, k, v, sinks[l]) @ w_attn_up[l]
        p = multiscale_pool(z, pool_w_mix[l], pool_scale[l]) @ w_pool_up[l]
        merged = jax.nn.sigmoid(g_attn) * a + jax.nn.sigmoid(g_pool) * p
        h = h + merged @ w_out[l]
        h = h + 0.5 * swiglu(rmsnorm(h, ffn2_norm[l]), ffn2_w_up[l], ffn2_w_down[l])
    return rmsnorm(h, final_norm)
```

```python
import functools

import jax
import jax.numpy as jnp
import numpy as np
from jax import lax
from jax.experimental import pallas as pl
from jax.experimental.pallas import tpu as pltpu

D_MODEL = 1024
SEQ = 8192
N_Q_HEADS = 16
N_KV_HEADS = 2
HEAD_DIM = 64
Q_PER_KV = N_Q_HEADS // N_KV_HEADS
WINDOW = 128
BLOCK = 128
ATTN_WIDTH = N_Q_HEADS * HEAD_DIM
KV_WIDTH = N_KV_HEADS * HEAD_DIM
POOL_WINDOWS = (2, 4, 8, 16)
POOL_WIDTH = 512
POOL_GROUP = POOL_WIDTH // len(POOL_WINDOWS)
D_FF = 2816
NORM_EPS = 1e-6

LANES = 128
MXU_DIM = 256
VMEM_LIMIT_BYTES = 56 * 1024 * 1024

FF_CHUNK = MXU_DIM
N_FF_CHUNKS = D_FF // FF_CHUNK
POOL_HIST = 16
PAIRS_PER_KV = Q_PER_KV // 2

F32 = jnp.float32
BF16 = jnp.bfloat16


def _resident(shape):
    nd = len(shape)
    return pl.BlockSpec(shape, lambda i: (0,) * nd, pipeline_mode=pl.Buffered(1))


def _rmsnorm(x, g):
    return x * lax.rsqrt(jnp.mean(x * x, axis=-1, keepdims=True) + NORM_EPS) * g


def _dot(a, b):
    return jnp.dot(a, b, preferred_element_type=F32)


def _dot_nt(a, b):
    return lax.dot_general(a, b, (((1,), (1,)), ((), ())), preferred_element_type=F32)


def _ffn_kernel(x_ref, g_ref, wup_ref, wdn_ref, *rest, final_norm):
    if final_norm:
        gf_ref, o_ref, xn_ref, acc_ref = rest
    else:
        o_ref, xn_ref, acc_ref = rest
    xn_ref[...] = _rmsnorm(x_ref[...], g_ref[...]).astype(BF16)
    acc_ref[...] = jnp.zeros_like(acc_ref)

    def chunk(c, carry):
        up = _dot(xn_ref[...], wup_ref[c])
        a = up[:, :FF_CHUNK]
        b = up[:, FF_CHUNK:]
        act = (a * jax.nn.sigmoid(a) * b).astype(BF16)
        acc_ref[...] += _dot(act, wdn_ref[c])
        return carry

    lax.fori_loop(0, N_FF_CHUNKS, chunk, 0)
    y = x_ref[...] + 0.5 * acc_ref[...]
    if final_norm:
        y = _rmsnorm(y, gf_ref[...])
    o_ref[...] = y


def _ffn(x, g, wup, wdn, gf, *, tm):
    n = x.shape[0]
    final_norm = gf is not None
    in_specs = [
        pl.BlockSpec((tm, D_MODEL), lambda i: (i, 0)),
        _resident((1, D_MODEL)),
        _resident((N_FF_CHUNKS, D_MODEL, 2 * FF_CHUNK)),
        _resident((N_FF_CHUNKS, FF_CHUNK, D_MODEL)),
    ]
    args = [x, g, wup, wdn]
    if final_norm:
        in_specs.append(_resident((1, D_MODEL)))
        args.append(gf)
    return pl.pallas_call(
        functools.partial(_ffn_kernel, final_norm=final_norm),
        grid=(n // tm,),
        in_specs=in_specs,
        out_specs=pl.BlockSpec((tm, D_MODEL), lambda i: (i, 0)),
        out_shape=jax.ShapeDtypeStruct((n, D_MODEL), F32),
        scratch_shapes=[pltpu.VMEM((tm, D_MODEL), BF16), pltpu.VMEM((tm, D_MODEL), F32)],
        compiler_params=pltpu.CompilerParams(
            dimension_semantics=("arbitrary",), vmem_limit_bytes=VMEM_LIMIT_BYTES),
        name="ffn_final" if final_norm else "ffn",
    )(*args)


_Q0 = 0
_KT0 = _Q0 + ATTN_WIDTH
_KB0 = _KT0 + N_KV_HEADS * LANES
_VT0 = _KB0 + N_KV_HEADS * LANES
_VB0 = _VT0 + N_KV_HEADS * LANES
_Z0 = _VB0 + N_KV_HEADS * LANES
_PROJ_W = _Z0 + POOL_WIDTH


def _mixer_kernel(sinks_ref, h_ref, gm_ref, wproj_ref, wgate_ref, wau_ref, wmix_ref, pscale_ref,
                  wpu_ref, wout_ref, bias_ref, o_ref,
                  u_scr, q_scr, kt_scr, kb_scr, vt_scr, vb_scr, z_scr, attn_scr, pool_scr, mrg_scr,
                  *, tm):
    i = pl.program_id(0)
    tiles_per_seq = SEQ // tm
    blocks_per_tile = tm // BLOCK
    tile_in_seq = i % tiles_per_seq
    seq_start = tile_in_seq == 0

    @pl.when(seq_start)
    def _():
        for scr in (kt_scr, kb_scr, vt_scr, vb_scr):
            scr[0:BLOCK, :] = jnp.zeros((BLOCK, scr.shape[1]), scr.dtype)
        z_scr[0:POOL_HIST, :] = jnp.zeros((POOL_HIST, POOL_WIDTH), F32)

    @pl.when(jnp.logical_not(seq_start))
    def _():
        for scr in (kt_scr, kb_scr, vt_scr, vb_scr):
            scr[0:BLOCK, :] = scr[tm:tm + BLOCK, :]
        z_scr[0:POOL_HIST, :] = z_scr[tm:tm + POOL_HIST, :]

    u = _rmsnorm(h_ref[...], gm_ref[...]).astype(BF16)
    u_scr[...] = u
    scale = HEAD_DIM ** -0.5
    q_scr[...] = (_dot(u, wproj_ref[:, _Q0:_KT0]) * scale).astype(BF16)
    kv = _dot(u, wproj_ref[:, _KT0:_Z0])
    kw = N_KV_HEADS * LANES
    kt_scr[BLOCK:, :] = kv[:, 0:kw].astype(BF16)
    kb_scr[BLOCK:, :] = kv[:, kw:2 * kw].astype(BF16)
    lane = lax.broadcasted_iota(jnp.int32, (tm + BLOCK, LANES), 1)
    ones_top = jnp.where(lane < HEAD_DIM, 1.0, 0.0).astype(BF16)
    ones_bot = jnp.where(lane >= HEAD_DIM, 1.0, 0.0).astype(BF16)
    for hk in range(N_KV_HEADS):
        c0 = hk * 2 * LANES
        vt_scr[BLOCK:, c0:c0 + LANES] = kv[:, 2 * kw + hk * LANES:2 * kw + (hk + 1) * LANES].astype(BF16)
        vb_scr[BLOCK:, c0:c0 + LANES] = kv[:, 3 * kw + hk * LANES:3 * kw + (hk + 1) * LANES].astype(BF16)
        vt_scr[:, c0 + LANES:c0 + 2 * LANES] = ones_top
        vb_scr[:, c0 + LANES:c0 + 2 * LANES] = ones_bot
    z_scr[POOL_HIST:, :] = _dot(u, wproj_ref[:, _Z0:_PROJ_W])

    half_lane = lax.broadcasted_iota(jnp.int32, (BLOCK, LANES), 1) < HEAD_DIM

    def attn_block(b, carry):
        r0 = pl.multiple_of(b * BLOCK, BLOCK)
        first_block = ((tile_in_seq * blocks_per_tile + b) == 0).astype(jnp.int32)
        for hk in range(N_KV_HEADS):
            kt = kt_scr[pl.ds(r0, 2 * BLOCK), hk * LANES:(hk + 1) * LANES]
            kb = kb_scr[pl.ds(r0, 2 * BLOCK), hk * LANES:(hk + 1) * LANES]
            vt = vt_scr[pl.ds(r0, 2 * BLOCK), hk * 2 * LANES:(hk + 1) * 2 * LANES]
            vb = vb_scr[pl.ds(r0, 2 * BLOCK), hk * 2 * LANES:(hk + 1) * 2 * LANES]
            for p in range(PAIRS_PER_KV):
                c0 = hk * Q_PER_KV * HEAD_DIM + p * LANES
                head0 = hk * Q_PER_KV + 2 * p
                qp = q_scr[pl.ds(r0, BLOCK), c0:c0 + LANES]
                s0 = _dot_nt(qp, kt) + bias_ref[first_block, head0]
                s1 = _dot_nt(qp, kb) + bias_ref[first_block, head0 + 1]
                sink0 = sinks_ref[head0]
                sink1 = sinks_ref[head0 + 1]
                m0 = jnp.maximum(jnp.max(s0, axis=-1, keepdims=True), sink0)
                m1 = jnp.maximum(jnp.max(s1, axis=-1, keepdims=True), sink1)
                e0 = jnp.exp(s0 - m0).astype(BF16)
                e1 = jnp.exp(s1 - m1).astype(BF16)
                pv = _dot(e0, vt) + _dot(e1, vb)
                sink_term = jnp.where(half_lane, jnp.exp(sink0 - m0), jnp.exp(sink1 - m1))
                out = pv[:, :LANES] / (pv[:, LANES:] + sink_term)
                attn_scr[pl.ds(r0, BLOCK), c0:c0 + LANES] = out.astype(BF16)
        return carry

    lax.fori_loop(0, blocks_per_tile, attn_block, 0)

    t = tile_in_seq * tm + lax.broadcasted_iota(jnp.int32, (tm, POOL_GROUP), 0)
    for gi, w in enumerate(POOL_WINDOWS):
        cs = slice(gi * POOL_GROUP, (gi + 1) * POOL_GROUP)
        zc = z_scr[POOL_HIST:POOL_HIST + tm, cs]
        acc = zc
        for j in range(1, w):
            acc = acc + z_scr[POOL_HIST - j:POOL_HIST - j + tm, cs]
        cnt = jnp.minimum(t + 1, w).astype(F32)
        pool_scr[:, cs] = (acc / cnt - zc).astype(BF16)
    mixed = (_dot(pool_scr[...], wmix_ref[...]) * pscale_ref[...]).astype(BF16)

    half = D_MODEL // 2
    for c in range(2):
        cs = slice(c * half, (c + 1) * half)
        cg = slice(D_MODEL + c * half, D_MODEL + (c + 1) * half)
        a = _dot(attn_scr[...], wau_ref[:, cs])
        p = _dot(mixed, wpu_ref[:, cs])
        ga = jax.nn.sigmoid(_dot(u_scr[...], wgate_ref[:, cs]))
        gp = jax.nn.sigmoid(_dot(u_scr[...], wgate_ref[:, cg]))
        mrg_scr[:, cs] = (ga * a + gp * p).astype(BF16)
    o_ref[...] = h_ref[...] + _dot(mrg_scr[...], wout_ref[...])


def _attention_bias():
    hidx = np.arange(1, N_Q_HEADS + 1, dtype=np.float32)
    slopes = (2.0 ** (-8.0 * hidx / N_Q_HEADS)).astype(np.float32)
    qi = np.arange(BLOCK)[:, None] + BLOCK
    kj = np.arange(2 * BLOCK)[None, :]
    dist = qi - kj
    valid = (dist >= 0) & (dist < WINDOW)
    valid_first = valid & (kj >= BLOCK)
    lin = -(slopes[:, None, None] * dist.astype(np.float32)[None])
    neg = np.float32(-np.inf)
    general = np.where(valid[None], lin, neg)
    first = np.where(valid_first[None], lin, neg)
    return jnp.asarray(np.stack([general, first]).astype(np.float32))


def _mixer(h, sinks, gm, wproj, wgate, wau, wmix, pscale, wpu, wout, bias, *, tm):
    n = h.shape[0]
    kvw = N_KV_HEADS * LANES
    return pl.pallas_call(
        functools.partial(_mixer_kernel, tm=tm),
        grid=(n // tm,),
        in_specs=[
            pl.BlockSpec(memory_space=pltpu.SMEM),
            pl.BlockSpec((tm, D_MODEL), lambda i: (i, 0)),
            _resident((1, D_MODEL)),
            _resident((D_MODEL, _PROJ_W)),
            _resident((D_MODEL, 2 * D_MODEL)),
            _resident((ATTN_WIDTH, D_MODEL)),
            _resident((POOL_WIDTH, POOL_WIDTH)),
            _resident((1, POOL_WIDTH)),
            _resident((POOL_WIDTH, D_MODEL)),
            _resident((D_MODEL, D_MODEL)),
            _resident((2, N_Q_HEADS, BLOCK, 2 * BLOCK)),
        ],
        out_specs=pl.BlockSpec((tm, D_MODEL), lambda i: (i, 0)),
        out_shape=jax.ShapeDtypeStruct((n, D_MODEL), F32),
        scratch_shapes=[
            pltpu.VMEM((tm, D_MODEL), BF16),
            pltpu.VMEM((tm, ATTN_WIDTH), BF16),
            pltpu.VMEM((tm + BLOCK, kvw), BF16),
            pltpu.VMEM((tm + BLOCK, kvw), BF16),
            pltpu.VMEM((tm + BLOCK, 2 * kvw), BF16),
            pltpu.VMEM((tm + BLOCK, 2 * kvw), BF16),
            pltpu.VMEM((tm + POOL_HIST, POOL_WIDTH), F32),
            pltpu.VMEM((tm, ATTN_WIDTH), BF16),
            pltpu.VMEM((tm, POOL_WIDTH), BF16),
            pltpu.VMEM((tm, D_MODEL), BF16),
        ],
        compiler_params=pltpu.CompilerParams(
            dimension_semantics=("arbitrary",), vmem_limit_bytes=VMEM_LIMIT_BYTES),
        name="mixer",
    )(sinks, h, gm, wproj, wgate, wau, wmix, pscale, wpu, wout, bias)


def _ffn_weights(w_up, w_down):
    wa = w_up[:, :D_FF].reshape(D_MODEL, N_FF_CHUNKS, FF_CHUNK)
    wb = w_up[:, D_FF:].reshape(D_MODEL, N_FF_CHUNKS, FF_CHUNK)
    wup = jnp.concatenate([wa, wb], axis=-1).transpose(1, 0, 2).astype(BF16)
    wdn = w_down.reshape(N_FF_CHUNKS, FF_CHUNK, D_MODEL).astype(BF16)
    return wup, wdn


def _mixer_weights(w_in, pool_w_mix):
    splits = np.cumsum([ATTN_WIDTH, KV_WIDTH, KV_WIDTH, POOL_WIDTH, D_MODEL])
    wq, wk, wv, wz, wga, wgp = jnp.split(w_in, [int(s) for s in splits], axis=-1)
    zeros = jnp.zeros((D_MODEL, HEAD_DIM), w_in.dtype)

    def top_bottom(w):
        heads = [w[:, hk * HEAD_DIM:(hk + 1) * HEAD_DIM] for hk in range(N_KV_HEADS)]
        top = jnp.concatenate([x for hd in heads for x in (hd, zeros)], axis=-1)
        bot = jnp.concatenate([x for hd in heads for x in (zeros, hd)], axis=-1)
        return top, bot

    kt, kb = top_bottom(wk)
    vt, vb = top_bottom(wv)
    wproj = jnp.concatenate([wq, kt, kb, vt, vb, wz], axis=-1).astype(BF16)
    wgate = jnp.concatenate([wga, wgp], axis=-1).astype(BF16)
    ng = len(POOL_WINDOWS)
    wmix = jnp.zeros((POOL_WIDTH, POOL_WIDTH), pool_w_mix.dtype)
    for gi in range(ng):
        sl = slice(gi * POOL_GROUP, (gi + 1) * POOL_GROUP)
        wmix = wmix.at[sl, sl].set(pool_w_mix[gi])
    return wproj, wgate, wmix.astype(BF16)


def kernel(x, ffn1_norm, ffn1_w_up, ffn1_w_down, mix_norm, w_in, sinks, w_attn_up, pool_w_mix, pool_scale,
           w_pool_up, w_out, ffn2_norm, ffn2_w_up, ffn2_w_down, final_norm):
    batch, seq, d = x.shape
    depth = ffn1_norm.shape[0]
    assert (seq, d) == (SEQ, D_MODEL)
    tm = 512
    h = x.reshape(batch * seq, d)
    bias = _attention_bias()
    for l in range(depth):
        wup1, wdn1 = _ffn_weights(ffn1_w_up[l], ffn1_w_down[l])
        wup2, wdn2 = _ffn_weights(ffn2_w_up[l], ffn2_w_down[l])
        wproj, wgate, wmix = _mixer_weights(w_in[l], pool_w_mix[l])
        h = _ffn(h, ffn1_norm[l].reshape(1, d), wup1, wdn1, None, tm=tm)
        h = _mixer(h, sinks[l], mix_norm[l].reshape(1, d), wproj, wgate, w_attn_up[l].astype(BF16), wmix,
                   pool_scale[l].reshape(1, POOL_WIDTH), w_pool_up[l].astype(BF16), w_out[l].astype(BF16),
                   bias, tm=tm)
        last = l == depth - 1
        h = _ffn(h, ffn2_norm[l].reshape(1, d), wup2, wdn2,
                 final_norm.reshape(1, d) if last else None, tm=tm)
    if depth == 0:
        raise ValueError("depth must be >= 1")
    return h.reshape(batch, seq, d)
```

```python
import functools
import math

import jax
import jax.numpy as jnp
import numpy as np
from jax import lax
from jax.experimental import pallas as pl
from jax.experimental.pallas import tpu as pltpu

D_MODEL = 1024
SEQ = 8192
N_Q_HEADS = 16
N_KV_HEADS = 2
HEAD_DIM = 64
Q_PER_KV = N_Q_HEADS // N_KV_HEADS
WINDOW = 128
BLOCK = 128
ATTN_WIDTH = N_Q_HEADS * HEAD_DIM
KV_WIDTH = N_KV_HEADS * HEAD_DIM
POOL_WINDOWS = (2, 4, 8, 16)
POOL_WIDTH = 512
POOL_GROUP = POOL_WIDTH // len(POOL_WINDOWS)
D_FF = 2816
NORM_EPS = 1e-6

LANES = 128
MXU_DIM = 256
VMEM_LIMIT_BYTES = 56 * 1024 * 1024

FFN_ROWS = 1024
MIXER_ROWS = 512
FF_CHUNK = MXU_DIM
N_FF_CHUNKS = D_FF // FF_CHUNK
POOL_HIST = 16
PAIRS_PER_KV = Q_PER_KV // 2
LOG2E = math.log2(math.e)

_K0 = ATTN_WIDTH
_V0 = _K0 + KV_WIDTH
_Z0 = _V0 + KV_WIDTH
_GA0 = _Z0 + POOL_WIDTH
_GP0 = _GA0 + D_MODEL
IN_WIDTH = _GP0 + D_MODEL

F32 = jnp.float32
BF16 = jnp.bfloat16


def _resident(shape):
    nd = len(shape)
    return pl.BlockSpec(shape, lambda i: (0,) * nd, pipeline_mode=pl.Buffered(1))


def _rmsnorm(x, g):
    return x * lax.rsqrt(jnp.mean(x * x, axis=-1, keepdims=True) + NORM_EPS) * g


def _dot(a, b):
    return jnp.dot(a, b, preferred_element_type=F32)


def _dot_nt(a, b):
    return lax.dot_general(a, b, (((1,), (1,)), ((), ())), preferred_element_type=F32)


def _ffn_kernel(x_ref, g_ref, wup_ref, wdn_ref, *rest, final_norm):
    if final_norm:
        gf_ref, o_ref, xn_ref, act_ref = rest
    else:
        o_ref, xn_ref, act_ref = rest
    xn_ref[...] = _rmsnorm(x_ref[...], g_ref[...]).astype(BF16)
    for c in range(N_FF_CHUNKS):
        cs = slice(c * FF_CHUNK, (c + 1) * FF_CHUNK)
        a = _dot(xn_ref[...], wup_ref[:, cs])
        b = _dot(xn_ref[...], wup_ref[:, D_FF + c * FF_CHUNK:D_FF + (c + 1) * FF_CHUNK])
        act_ref[:, cs] = (a * jax.nn.sigmoid(a) * b).astype(BF16)
    y = x_ref[...] + 0.5 * _dot(act_ref[...], wdn_ref[...])
    if final_norm:
        y = _rmsnorm(y, gf_ref[...])
    o_ref[...] = y


def _ffn(x, g, wup, wdn, gf, *, tm):
    n = x.shape[0]
    final_norm = gf is not None
    in_specs = [
        pl.BlockSpec((tm, D_MODEL), lambda i: (i, 0)),
        _resident((1, D_MODEL)),
        _resident((D_MODEL, 2 * D_FF)),
        _resident((D_FF, D_MODEL)),
    ]
    args = [x, g, wup, wdn]
    if final_norm:
        in_specs.append(_resident((1, D_MODEL)))
        args.append(gf)
    return pl.pallas_call(
        functools.partial(_ffn_kernel, final_norm=final_norm),
        grid=(n // tm,),
        in_specs=in_specs,
        out_specs=pl.BlockSpec((tm, D_MODEL), lambda i: (i, 0)),
        out_shape=jax.ShapeDtypeStruct((n, D_MODEL), F32),
        scratch_shapes=[pltpu.VMEM((tm, D_MODEL), BF16), pltpu.VMEM((tm, D_FF), BF16)],
        compiler_params=pltpu.CompilerParams(
            dimension_semantics=("arbitrary",), vmem_limit_bytes=VMEM_LIMIT_BYTES),
        name="ffn_final" if final_norm else "ffn",
    )(*args)


def _mixer_kernel(sinks_ref, h_ref, gm_ref, win_ref, wau_ref, wmix_ref, pscale_ref,
                  wpu_ref, wout_ref, bias_ref, o_ref,
                  u_scr, q_scr, kt_scr, kb_scr, vt_scr, vb_scr, z_scr, attn_scr, pool_scr, mrg_scr,
                  *, tm):
    i = pl.program_id(0)
    tiles_per_seq = SEQ // tm
    blocks_per_tile = tm // BLOCK
    tile_in_seq = i % tiles_per_seq
    seq_start = tile_in_seq == 0

    @pl.when(seq_start)
    def _():
        for scr in (kt_scr, kb_scr, vt_scr, vb_scr):
            scr[0:BLOCK, :] = jnp.zeros((BLOCK, scr.shape[1]), scr.dtype)
        z_scr[0:POOL_HIST, :] = jnp.zeros((POOL_HIST, POOL_WIDTH), F32)

    @pl.when(jnp.logical_not(seq_start))
    def _():
        for scr in (kt_scr, kb_scr, vt_scr, vb_scr):
            scr[0:BLOCK, :] = scr[tm:tm + BLOCK, :]
        z_scr[0:POOL_HIST, :] = z_scr[tm:tm + POOL_HIST, :]

    u = _rmsnorm(h_ref[...], gm_ref[...]).astype(BF16)
    u_scr[...] = u
    q_scr[...] = (_dot(u, win_ref[:, 0:_K0]) * (HEAD_DIM ** -0.5 * LOG2E)).astype(BF16)

    kv = _dot(u, win_ref[:, _K0:_Z0])
    low = lax.broadcasted_iota(jnp.int32, (tm, LANES), 1) < HEAD_DIM

    def top_bottom(x):
        xr = pltpu.roll(x, HEAD_DIM, 1)
        zero = jnp.zeros_like(x)
        tops = (jnp.where(low, x, zero), jnp.where(low, xr, zero))
        bots = (jnp.where(low, zero, xr), jnp.where(low, zero, x))
        return tops, bots

    k_tops, k_bots = top_bottom(kv[:, :LANES])
    v_tops, v_bots = top_bottom(kv[:, LANES:])
    lane = lax.broadcasted_iota(jnp.int32, (tm + BLOCK, LANES), 1)
    ones_top = jnp.where(lane < HEAD_DIM, 1.0, 0.0).astype(BF16)
    ones_bot = jnp.where(lane >= HEAD_DIM, 1.0, 0.0).astype(BF16)
    for hk in range(N_KV_HEADS):
        kt_scr[BLOCK:, hk * LANES:(hk + 1) * LANES] = k_tops[hk].astype(BF16)
        kb_scr[BLOCK:, hk * LANES:(hk + 1) * LANES] = k_bots[hk].astype(BF16)
        c0 = hk * 2 * LANES
        vt_scr[BLOCK:, c0:c0 + LANES] = v_tops[hk].astype(BF16)
        vb_scr[BLOCK:, c0:c0 + LANES] = v_bots[hk].astype(BF16)
        vt_scr[:, c0 + LANES:c0 + 2 * LANES] = ones_top
        vb_scr[:, c0 + LANES:c0 + 2 * LANES] = ones_bot
    z_scr[POOL_HIST:, :] = _dot(u, win_ref[:, _Z0:_GA0])

    half_lane = lax.broadcasted_iota(jnp.int32, (BLOCK, LANES), 1) < HEAD_DIM

    def attn_block(b, carry):
        r0 = pl.multiple_of(b * BLOCK, BLOCK)
        first_block = ((tile_in_seq * blocks_per_tile + b) == 0).astype(jnp.int32)
        for hk in range(N_KV_HEADS):
            kt = kt_scr[pl.ds(r0, 2 * BLOCK), hk * LANES:(hk + 1) * LANES]
            kb = kb_scr[pl.ds(r0, 2 * BLOCK), hk * LANES:(hk + 1) * LANES]
            vt = vt_scr[pl.ds(r0, 2 * BLOCK), hk * 2 * LANES:(hk + 1) * 2 * LANES]
            vb = vb_scr[pl.ds(r0, 2 * BLOCK), hk * 2 * LANES:(hk + 1) * 2 * LANES]
            q0 = hk * Q_PER_KV * HEAD_DIM
            qs = jnp.concatenate(
                [q_scr[pl.ds(r0, BLOCK), q0 + p * LANES:q0 + (p + 1) * LANES] for p in range(PAIRS_PER_KV)],
                axis=0)
            s = _dot_nt(qs, jnp.concatenate([kt, kb], axis=0)) + bias_ref[first_block, hk]
            s_even = s[:, :2 * BLOCK]
            s_odd = s[:, 2 * BLOCK:]
            e_even, e_odd, sink_terms = [], [], []
            for p in range(PAIRS_PER_KV):
                rows = slice(p * BLOCK, (p + 1) * BLOCK)
                head0 = hk * Q_PER_KV + 2 * p
                sink0 = sinks_ref[head0] * LOG2E
                sink1 = sinks_ref[head0 + 1] * LOG2E
                s0 = s_even[rows]
                s1 = s_odd[rows]
                m0 = jnp.maximum(jnp.max(s0, axis=-1, keepdims=True), sink0)
                m1 = jnp.maximum(jnp.max(s1, axis=-1, keepdims=True), sink1)
                e_even.append(jnp.exp2(s0 - m0).astype(BF16))
                e_odd.append(jnp.exp2(s1 - m1).astype(BF16))
                sink_terms.append(jnp.where(half_lane, jnp.exp2(sink0 - m0), jnp.exp2(sink1 - m1)))
            pv = (_dot(jnp.concatenate(e_even, axis=0), vt)
                  + _dot(jnp.concatenate(e_odd, axis=0), vb))
            for p in range(PAIRS_PER_KV):
                rows = slice(p * BLOCK, (p + 1) * BLOCK)
                out = pv[rows, :LANES] / (pv[rows, LANES:] + sink_terms[p])
                attn_scr[pl.ds(r0, BLOCK), q0 + p * LANES:q0 + (p + 1) * LANES] = out.astype(BF16)
        return carry

    lax.fori_loop(0, blocks_per_tile, attn_block, 0, unroll=True)

    t = tile_in_seq * tm + lax.broadcasted_iota(jnp.int32, (tm, POOL_GROUP), 0)
    for gi, w in enumerate(POOL_WINDOWS):
        cs = slice(gi * POOL_GROUP, (gi + 1) * POOL_GROUP)
        zc = z_scr[POOL_HIST:POOL_HIST + tm, cs]
        acc = zc
        for j in range(1, w):
            acc = acc + z_scr[POOL_HIST - j:POOL_HIST - j + tm, cs]
        cnt = jnp.minimum(t + 1, w).astype(F32)
        pool_scr[:, cs] = (acc / cnt - zc).astype(BF16)
    mixed = (_dot(pool_scr[...], wmix_ref[...]) * pscale_ref[...]).astype(BF16)

    half = D_MODEL // 2
    for c in range(2):
        cs = slice(c * half, (c + 1) * half)
        a = _dot(attn_scr[...], wau_ref[:, cs])
        p = _dot(mixed, wpu_ref[:, cs])
        ga = jax.nn.sigmoid(_dot(u_scr[...], win_ref[:, _GA0 + c * half:_GA0 + (c + 1) * half]))
        gp = jax.nn.sigmoid(_dot(u_scr[...], win_ref[:, _GP0 + c * half:_GP0 + (c + 1) * half]))
        mrg_scr[:, cs] = (ga * a + gp * p).astype(BF16)
    o_ref[...] = h_ref[...] + _dot(mrg_scr[...], wout_ref[...])


def _attention_bias():
    hidx = np.arange(1, N_Q_HEADS + 1, dtype=np.float32)
    slopes = (2.0 ** (-8.0 * hidx / N_Q_HEADS)).astype(np.float32)
    qi = np.arange(BLOCK)[:, None] + BLOCK
    kj = np.arange(2 * BLOCK)[None, :]
    dist = qi - kj
    valid = (dist >= 0) & (dist < WINDOW)
    valid_first = valid & (kj >= BLOCK)
    lin = -(slopes[:, None, None] * dist.astype(np.float32)[None]).astype(np.float64) * LOG2E
    tables = []
    for vis in (valid, valid_first):
        per_head = np.where(vis[None], lin, -np.inf)
        per_head = per_head.reshape(N_KV_HEADS, PAIRS_PER_KV, 2, BLOCK, 2 * BLOCK)
        stacked = per_head.transpose(0, 1, 3, 2, 4).reshape(N_KV_HEADS, PAIRS_PER_KV * BLOCK, 4 * BLOCK)
        tables.append(stacked)
    return jnp.asarray(np.stack(tables).astype(np.float32))


def _mixer(h, sinks, gm, win, wau, wmix, pscale, wpu, wout, bias, *, tm):
    n = h.shape[0]
    kvw = N_KV_HEADS * LANES
    return pl.pallas_call(
        functools.partial(_mixer_kernel, tm=tm),
        grid=(n // tm,),
        in_specs=[
            pl.BlockSpec(memory_space=pltpu.SMEM),
            pl.BlockSpec((tm, D_MODEL), lambda i: (i, 0)),
            _resident((1, D_MODEL)),
            _resident((D_MODEL, IN_WIDTH)),
            _resident((ATTN_WIDTH, D_MODEL)),
            _resident((POOL_WIDTH, POOL_WIDTH)),
            _resident((1, POOL_WIDTH)),
            _resident((POOL_WIDTH, D_MODEL)),
            _resident((D_MODEL, D_MODEL)),
            _resident((2, N_KV_HEADS, PAIRS_PER_KV * BLOCK, 4 * BLOCK)),
        ],
        out_specs=pl.BlockSpec((tm, D_MODEL), lambda i: (i, 0)),
        out_shape=jax.ShapeDtypeStruct((n, D_MODEL), F32),
        scratch_shapes=[
            pltpu.VMEM((tm, D_MODEL), BF16),
            pltpu.VMEM((tm, ATTN_WIDTH), BF16),
            pltpu.VMEM((tm + BLOCK, kvw), BF16),
            pltpu.VMEM((tm + BLOCK, kvw), BF16),
            pltpu.VMEM((tm + BLOCK, 2 * kvw), BF16),
            pltpu.VMEM((tm + BLOCK, 2 * kvw), BF16),
            pltpu.VMEM((tm + POOL_HIST, POOL_WIDTH), F32),
            pltpu.VMEM((tm, ATTN_WIDTH), BF16),
            pltpu.VMEM((tm, POOL_WIDTH), BF16),
            pltpu.VMEM((tm, D_MODEL), BF16),
        ],
        compiler_params=pltpu.CompilerParams(
            dimension_semantics=("arbitrary",), vmem_limit_bytes=VMEM_LIMIT_BYTES),
        name="mixer",
    )(sinks, h, gm, win, wau, wmix, pscale, wpu, wout, bias)


def _block_diagonal(blocks):
    g, n, _ = blocks.shape
    out = jnp.zeros((g * n, g * n), blocks.dtype)
    for gi in range(g):
        out = out.at[gi * n:(gi + 1) * n, gi * n:(gi + 1) * n].set(blocks[gi])
    return out


def kernel(x, ffn1_norm, ffn1_w_up, ffn1_w_down, mix_norm, w_in, sinks, w_attn_up, pool_w_mix, pool_scale,
           w_pool_up, w_out, ffn2_norm, ffn2_w_up, ffn2_w_down, final_norm):
    batch, seq, d = x.shape
    depth = ffn1_norm.shape[0]
    assert (seq, d) == (SEQ, D_MODEL)
    h = x.reshape(batch * seq, d)
    bias = _attention_bias()
    for l in range(depth):
        h = _ffn(h, ffn1_norm[l].reshape(1, d), ffn1_w_up[l].astype(BF16), ffn1_w_down[l].astype(BF16), None,
                 tm=FFN_ROWS)
        h = _mixer(h, sinks[l], mix_norm[l].reshape(1, d), w_in[l].astype(BF16), w_attn_up[l].astype(BF16),
                   _block_diagonal(pool_w_mix[l]).astype(BF16), pool_scale[l].reshape(1, POOL_WIDTH),
                   w_pool_up[l].astype(BF16), w_out[l].astype(BF16), bias, tm=MIXER_ROWS)
        last = l == depth - 1
        h = _ffn(h, ffn2_norm[l].reshape(1, d), ffn2_w_up[l].astype(BF16), ffn2_w_down[l].astype(BF16),
                 final_norm.reshape(1, d) if last else None, tm=FFN_ROWS)
    return h.reshape(batch, seq, d)
```

```python
import functools
import math

import jax
import jax.numpy as jnp
import numpy as np
from jax import lax
from jax.experimental import pallas as pl
from jax.experimental.pallas import tpu as pltpu

D_MODEL = 1024
SEQ = 8192
N_Q_HEADS = 16
N_KV_HEADS = 2
HEAD_DIM = 64
Q_PER_KV = N_Q_HEADS // N_KV_HEADS
WINDOW = 128
BLOCK = 128
ATTN_WIDTH = N_Q_HEADS * HEAD_DIM
KV_WIDTH = N_KV_HEADS * HEAD_DIM
POOL_WINDOWS = (2, 4, 8, 16)
POOL_WIDTH = 512
POOL_GROUP = POOL_WIDTH // len(POOL_WINDOWS)
D_FF = 2816
NORM_EPS = 1e-6

LANES = 128
MXU_DIM = 256
VMEM_LIMIT_BYTES = 56 * 1024 * 1024

FFN_ROWS = 1024
MIXER_ROWS = 512
FF_CHUNK = MXU_DIM
N_FF_CHUNKS = D_FF // FF_CHUNK
POOL_HIST = 16
PAIRS_PER_KV = Q_PER_KV // 2
GATE_ROWS = MXU_DIM
GATE_COLS = 2 * MXU_DIM
LOG2E = math.log2(math.e)

_K0 = ATTN_WIDTH
_V0 = _K0 + KV_WIDTH
_Z0 = _V0 + KV_WIDTH
_GA0 = _Z0 + POOL_WIDTH
_GP0 = _GA0 + D_MODEL
IN_WIDTH = _GP0 + D_MODEL

F32 = jnp.float32
BF16 = jnp.bfloat16


def _resident(shape):
    nd = len(shape)
    return pl.BlockSpec(shape, lambda i: (0,) * nd, pipeline_mode=pl.Buffered(1))


def _rmsnorm(x, g):
    return x * lax.rsqrt(jnp.mean(x * x, axis=-1, keepdims=True) + NORM_EPS) * g


def _dot(a, b):
    return jnp.dot(a, b, preferred_element_type=F32)


def _dot_nt(a, b):
    return lax.dot_general(a, b, (((1,), (1,)), ((), ())), preferred_element_type=F32)


def _ffn_kernel(x_ref, g_ref, wup_ref, wdn_ref, *rest, final_norm):
    if final_norm:
        gf_ref, o_ref, xn_ref, act_ref = rest
    else:
        o_ref, xn_ref, act_ref = rest
    xn_ref[...] = _rmsnorm(x_ref[...], g_ref[...]).astype(BF16)
    for c in range(N_FF_CHUNKS):
        cs = slice(c * FF_CHUNK, (c + 1) * FF_CHUNK)
        a = _dot(xn_ref[...], wup_ref[:, cs])
        b = _dot(xn_ref[...], wup_ref[:, D_FF + c * FF_CHUNK:D_FF + (c + 1) * FF_CHUNK])
        act_ref[:, cs] = (a * jax.nn.sigmoid(a) * b).astype(BF16)
    y = x_ref[...] + 0.5 * _dot(act_ref[...], wdn_ref[...])
    if final_norm:
        y = _rmsnorm(y, gf_ref[...])
    o_ref[...] = y


def _ffn(x, g, wup, wdn, gf, *, tm):
    n = x.shape[0]
    final_norm = gf is not None
    in_specs = [
        pl.BlockSpec((tm, D_MODEL), lambda i: (i, 0)),
        _resident((1, D_MODEL)),
        _resident((D_MODEL, 2 * D_FF)),
        _resident((D_FF, D_MODEL)),
    ]
    args = [x, g, wup, wdn]
    if final_norm:
        in_specs.append(_resident((1, D_MODEL)))
        args.append(gf)
    return pl.pallas_call(
        functools.partial(_ffn_kernel, final_norm=final_norm),
        grid=(n // tm,),
        in_specs=in_specs,
        out_specs=pl.BlockSpec((tm, D_MODEL), lambda i: (i, 0)),
        out_shape=jax.ShapeDtypeStruct((n, D_MODEL), F32),
        scratch_shapes=[pltpu.VMEM((tm, D_MODEL), BF16), pltpu.VMEM((tm, D_FF), BF16)],
        compiler_params=pltpu.CompilerParams(
            dimension_semantics=("arbitrary",), vmem_limit_bytes=VMEM_LIMIT_BYTES),
        name="ffn_final" if final_norm else "ffn",
    )(*args)


def _mixer_kernel(sinks_ref, h_ref, gm_ref, win_ref, wau_ref, wmix_ref, pscale_ref,
                  wpu_ref, wout_ref, bias_ref, o_ref,
                  u_scr, q_scr, kT_scr, vt_scr, vb_scr, z_scr, attn_scr, pool_scr, mrg_scr, gate_scr,
                  *, tm):
    i = pl.program_id(0)
    tiles_per_seq = SEQ // tm
    blocks_per_tile = tm // BLOCK
    gate_row_groups = tm // GATE_ROWS
    assert gate_row_groups * (2 * D_MODEL // GATE_COLS) == blocks_per_tile * N_KV_HEADS
    tile_in_seq = i % tiles_per_seq
    seq_start = tile_in_seq == 0

    @pl.when(seq_start)
    def _():
        kT_scr[:, 0:BLOCK] = jnp.zeros((LANES, BLOCK), BF16)
        for scr in (vt_scr, vb_scr):
            scr[0:BLOCK, :] = jnp.zeros((BLOCK, scr.shape[1]), scr.dtype)
        z_scr[0:POOL_HIST, :] = jnp.zeros((POOL_HIST, POOL_WIDTH), F32)

    @pl.when(jnp.logical_not(seq_start))
    def _():
        kT_scr[:, 0:BLOCK] = kT_scr[:, tm:tm + BLOCK]
        for scr in (vt_scr, vb_scr):
            scr[0:BLOCK, :] = scr[tm:tm + BLOCK, :]
        z_scr[0:POOL_HIST, :] = z_scr[tm:tm + POOL_HIST, :]

    u = _rmsnorm(h_ref[...], gm_ref[...]).astype(BF16)
    u_scr[...] = u
    q_scr[...] = (_dot(u, win_ref[:, 0:_K0]) * (HEAD_DIM ** -0.5 * LOG2E)).astype(BF16)

    kv = _dot(u, win_ref[:, _K0:_Z0])
    kT_scr[:, BLOCK:] = kv[:, :LANES].T.astype(BF16)

    low = lax.broadcasted_iota(jnp.int32, (tm, LANES), 1) < HEAD_DIM
    v01 = kv[:, LANES:]
    v10 = pltpu.roll(v01, HEAD_DIM, 1)
    zero = jnp.zeros_like(v01)
    v_tops = (jnp.where(low, v01, zero), jnp.where(low, v10, zero))
    v_bots = (jnp.where(low, zero, v10), jnp.where(low, zero, v01))
    lane = lax.broadcasted_iota(jnp.int32, (tm + BLOCK, LANES), 1)
    ones_top = jnp.where(lane < HEAD_DIM, 1.0, 0.0).astype(BF16)
    ones_bot = jnp.where(lane >= HEAD_DIM, 1.0, 0.0).astype(BF16)
    for hk in range(N_KV_HEADS):
        c0 = hk * 2 * LANES
        vt_scr[BLOCK:, c0:c0 + LANES] = v_tops[hk].astype(BF16)
        vb_scr[BLOCK:, c0:c0 + LANES] = v_bots[hk].astype(BF16)
        vt_scr[:, c0 + LANES:c0 + 2 * LANES] = ones_top
        vb_scr[:, c0 + LANES:c0 + 2 * LANES] = ones_bot
    z_scr[POOL_HIST:, :] = _dot(u, win_ref[:, _Z0:_GA0])

    half_lane = lax.broadcasted_iota(jnp.int32, (BLOCK, LANES), 1) < HEAD_DIM
    kzero = jnp.zeros((HEAD_DIM, 2 * BLOCK), BF16)

    for b in range(blocks_per_tile):
        r0 = b * BLOCK
        first_block = seq_start.astype(jnp.int32) if b == 0 else 0
        for hk in range(N_KV_HEADS):
            kTb = kT_scr[hk * HEAD_DIM:(hk + 1) * HEAD_DIM, r0:r0 + 2 * BLOCK]
            k_even_odd = jnp.concatenate(
                [jnp.concatenate([kTb, kzero], axis=1), jnp.concatenate([kzero, kTb], axis=1)], axis=0)
            vt = vt_scr[r0:r0 + 2 * BLOCK, hk * 2 * LANES:(hk + 1) * 2 * LANES]
            vb = vb_scr[r0:r0 + 2 * BLOCK, hk * 2 * LANES:(hk + 1) * 2 * LANES]
            q0 = hk * Q_PER_KV * HEAD_DIM
            qs = jnp.concatenate(
                [q_scr[r0:r0 + BLOCK, q0 + p * LANES:q0 + (p + 1) * LANES] for p in range(PAIRS_PER_KV)],
                axis=0)
            s = _dot(qs, k_even_odd) + bias_ref[first_block, hk]
            s_even = s[:, :2 * BLOCK]
            s_odd = s[:, 2 * BLOCK:]
            step = b * N_KV_HEADS + hk
            grow = slice((step % gate_row_groups) * GATE_ROWS, (step % gate_row_groups + 1) * GATE_ROWS)
            gcol = (step // gate_row_groups) * GATE_COLS
            gate_scr[grow, gcol:gcol + GATE_COLS] = _dot(
                u_scr[grow, :], win_ref[:, _GA0 + gcol:_GA0 + gcol + GATE_COLS])
            e_even, e_odd, sink_terms = [], [], []
            for p in range(PAIRS_PER_KV):
                rows = slice(p * BLOCK, (p + 1) * BLOCK)
                head0 = hk * Q_PER_KV + 2 * p
                sink0 = sinks_ref[head0] * LOG2E
                sink1 = sinks_ref[head0 + 1] * LOG2E
                s0 = s_even[rows]
                s1 = s_odd[rows]
                m0 = jnp.maximum(jnp.max(s0, axis=-1, keepdims=True), sink0)
                m1 = jnp.maximum(jnp.max(s1, axis=-1, keepdims=True), sink1)
                e_even.append(jnp.exp2(s0 - m0).astype(BF16))
                e_odd.append(jnp.exp2(s1 - m1).astype(BF16))
                sink_terms.append(jnp.where(half_lane, jnp.exp2(sink0 - m0), jnp.exp2(sink1 - m1)))
            pv = (_dot(jnp.concatenate(e_even, axis=0), vt)
                  + _dot(jnp.concatenate(e_odd, axis=0), vb))
            for p in range(PAIRS_PER_KV):
                rows = slice(p * BLOCK, (p + 1) * BLOCK)
                out = pv[rows, :LANES] / (pv[rows, LANES:] + sink_terms[p])
                attn_scr[r0:r0 + BLOCK, q0 + p * LANES:q0 + (p + 1) * LANES] = out.astype(BF16)

    t = tile_in_seq * tm + lax.broadcasted_iota(jnp.int32, (tm, POOL_GROUP), 0)
    for gi, w in enumerate(POOL_WINDOWS):
        cs = slice(gi * POOL_GROUP, (gi + 1) * POOL_GROUP)
        acc = z_scr[:, cs]
        span = 1
        while span < w:
            acc = acc + pltpu.roll(acc, span, 0)
            span *= 2
        cnt = jnp.minimum(t + 1, w).astype(F32)
        pool_scr[:, cs] = (acc[POOL_HIST:] / cnt - z_scr[POOL_HIST:, cs]).astype(BF16)
    mixed = (_dot(pool_scr[...], wmix_ref[...]) * pscale_ref[...]).astype(BF16)

    half = D_MODEL // 2
    for c in range(2):
        cs = slice(c * half, (c + 1) * half)
        a = _dot(attn_scr[...], wau_ref[:, cs])
        p = _dot(mixed, wpu_ref[:, cs])
        ga = jax.nn.sigmoid(gate_scr[:, c * half:(c + 1) * half])
        gp = jax.nn.sigmoid(gate_scr[:, D_MODEL + c * half:D_MODEL + (c + 1) * half])
        mrg_scr[:, cs] = (ga * a + gp * p).astype(BF16)
    o_ref[...] = h_ref[...] + _dot(mrg_scr[...], wout_ref[...])


def _attention_bias():
    hidx = np.arange(1, N_Q_HEADS + 1, dtype=np.float32)
    slopes = (2.0 ** (-8.0 * hidx / N_Q_HEADS)).astype(np.float32)
    qi = np.arange(BLOCK)[:, None] + BLOCK
    kj = np.arange(2 * BLOCK)[None, :]
    dist = qi - kj
    valid = (dist >= 0) & (dist < WINDOW)
    valid_first = valid & (kj >= BLOCK)
    lin = -(slopes[:, None, None] * dist.astype(np.float32)[None]).astype(np.float64) * LOG2E
    tables = []
    for vis in (valid, valid_first):
        per_head = np.where(vis[None], lin, -np.inf)
        per_head = per_head.reshape(N_KV_HEADS, PAIRS_PER_KV, 2, BLOCK, 2 * BLOCK)
        stacked = per_head.transpose(0, 1, 3, 2, 4).reshape(N_KV_HEADS, PAIRS_PER_KV * BLOCK, 4 * BLOCK)
        tables.append(stacked)
    return jnp.asarray(np.stack(tables).astype(np.float32))


def _mixer(h, sinks, gm, win, wau, wmix, pscale, wpu, wout, bias, *, tm):
    n = h.shape[0]
    kvw = N_KV_HEADS * LANES
    return pl.pallas_call(
        functools.partial(_mixer_kernel, tm=tm),
        grid=(n // tm,),
        in_specs=[
            pl.BlockSpec(memory_space=pltpu.SMEM),
            pl.BlockSpec((tm, D_MODEL), lambda i: (i, 0)),
            _resident((1, D_MODEL)),
            _resident((D_MODEL, IN_WIDTH)),
            _resident((ATTN_WIDTH, D_MODEL)),
            _resident((POOL_WIDTH, POOL_WIDTH)),
            _resident((1, POOL_WIDTH)),
            _resident((POOL_WIDTH, D_MODEL)),
            _resident((D_MODEL, D_MODEL)),
            _resident((2, N_KV_HEADS, PAIRS_PER_KV * BLOCK, 4 * BLOCK)),
        ],
        out_specs=pl.BlockSpec((tm, D_MODEL), lambda i: (i, 0)),
        out_shape=jax.ShapeDtypeStruct((n, D_MODEL), F32),
        scratch_shapes=[
            pltpu.VMEM((tm, D_MODEL), BF16),
            pltpu.VMEM((tm, ATTN_WIDTH), BF16),
            pltpu.VMEM((LANES, tm + BLOCK), BF16),
            pltpu.VMEM((tm + BLOCK, 2 * kvw), BF16),
            pltpu.VMEM((tm + BLOCK, 2 * kvw), BF16),
            pltpu.VMEM((tm + POOL_HIST, POOL_WIDTH), F32),
            pltpu.VMEM((tm, ATTN_WIDTH), BF16),
            pltpu.VMEM((tm, POOL_WIDTH), BF16),
            pltpu.VMEM((tm, D_MODEL), BF16),
            pltpu.VMEM((tm, 2 * D_MODEL), F32),
        ],
        compiler_params=pltpu.CompilerParams(
            dimension_semantics=("arbitrary",), vmem_limit_bytes=VMEM_LIMIT_BYTES),
        name="mixer",
    )(sinks, h, gm, win, wau, wmix, pscale, wpu, wout, bias)


def _block_diagonal(blocks):
    g, n, _ = blocks.shape
    out = jnp.zeros((g * n, g * n), blocks.dtype)
    for gi in range(g):
        out = out.at[gi * n:(gi + 1) * n, gi * n:(gi + 1) * n].set(blocks[gi])
    return out


def kernel(x, ffn1_norm, ffn1_w_up, ffn1_w_down, mix_norm, w_in, sinks, w_attn_up, pool_w_mix, pool_scale,
           w_pool_up, w_out, ffn2_norm, ffn2_w_up, ffn2_w_down, final_norm):
    batch, seq, d = x.shape
    depth = ffn1_norm.shape[0]
    assert (seq, d) == (SEQ, D_MODEL)
    h = x.reshape(batch * seq, d)
    bias = _attention_bias()
    for l in range(depth):
        h = _ffn(h, ffn1_norm[l].reshape(1, d), ffn1_w_up[l].astype(BF16), ffn1_w_down[l].astype(BF16), None,
                 tm=FFN_ROWS)
        h = _mixer(h, sinks[l], mix_norm[l].reshape(1, d), w_in[l].astype(BF16), w_attn_up[l].astype(BF16),
                   _block_diagonal(pool_w_mix[l]).astype(BF16), pool_scale[l].reshape(1, POOL_WIDTH),
                   w_pool_up[l].astype(BF16), w_out[l].astype(BF16), bias, tm=MIXER_ROWS)
        last = l == depth - 1
        h = _ffn(h, ffn2_norm[l].reshape(1, d), ffn2_w_up[l].astype(BF16), ffn2_w_down[l].astype(BF16),
                 final_norm.reshape(1, d) if last else None, tm=FFN_ROWS)
    return h.reshape(batch, seq, d)
```

```python
import functools
import math

import jax
import jax.numpy as jnp
import numpy as np
from jax import lax
from jax.experimental import pallas as pl
from jax.experimental.pallas import tpu as pltpu

D_MODEL = 1024
SEQ = 8192
N_Q_HEADS = 16
N_KV_HEADS = 2
HEAD_DIM = 64
Q_PER_KV = N_Q_HEADS // N_KV_HEADS
WINDOW = 128
BLOCK = 128
ATTN_WIDTH = N_Q_HEADS * HEAD_DIM
KV_WIDTH = N_KV_HEADS * HEAD_DIM
POOL_WINDOWS = (2, 4, 8, 16)
POOL_WIDTH = 512
POOL_GROUP = POOL_WIDTH // len(POOL_WINDOWS)
D_FF = 2816
NORM_EPS = 1e-6

LANES = 128
MXU_DIM = 256
VMEM_LIMIT_BYTES = 56 * 1024 * 1024

FFN_ROWS = 1024
MIXER_ROWS = 1024
ROW_GROUP = MXU_DIM
FF_CHUNK = MXU_DIM
N_FF_CHUNKS = D_FF // FF_CHUNK
POOL_HIST = 16
PAIRS_PER_KV = Q_PER_KV // 2
GATE_ROWS = MXU_DIM
GATE_COLS = 2 * MXU_DIM
LOG2E = math.log2(math.e)

_K0 = ATTN_WIDTH
_V0 = _K0 + KV_WIDTH
_Z0 = _V0 + KV_WIDTH
_GA0 = _Z0 + POOL_WIDTH
_GP0 = _GA0 + D_MODEL
IN_WIDTH = _GP0 + D_MODEL

F32 = jnp.float32
BF16 = jnp.bfloat16


def _resident(shape):
    nd = len(shape)
    return pl.BlockSpec(shape, lambda i: (0,) * nd, pipeline_mode=pl.Buffered(1))


def _rmsnorm(x, g):
    return x * lax.rsqrt(jnp.mean(x * x, axis=-1, keepdims=True) + NORM_EPS) * g


def _dot(a, b):
    return jnp.dot(a, b, preferred_element_type=F32)


def _dot_nt(a, b):
    return lax.dot_general(a, b, (((1,), (1,)), ((), ())), preferred_element_type=F32)


def _ffn_kernel(x_ref, g_ref, wup_ref, wdn_ref, *rest, final_norm):
    if final_norm:
        gf_ref, o_ref, xn_ref, act_ref = rest
    else:
        o_ref, xn_ref, act_ref = rest
    tm = x_ref.shape[0]

    def swiglu_chunk(c, rows):
        cs = slice(c * FF_CHUNK, (c + 1) * FF_CHUNK)
        a = _dot(xn_ref[rows, :], wup_ref[:, cs])
        b = _dot(xn_ref[rows, :], wup_ref[:, D_FF + c * FF_CHUNK:D_FF + (c + 1) * FF_CHUNK])
        act_ref[rows, cs] = (a * jax.nn.sigmoid(a) * b).astype(BF16)

    for r in range(tm // ROW_GROUP):
        rows = slice(r * ROW_GROUP, (r + 1) * ROW_GROUP)
        xn_ref[rows, :] = _rmsnorm(x_ref[rows, :], g_ref[...]).astype(BF16)
        swiglu_chunk(0, rows)
    for c in range(1, N_FF_CHUNKS):
        swiglu_chunk(c, slice(None))
    for r in range(tm // ROW_GROUP):
        rows = slice(r * ROW_GROUP, (r + 1) * ROW_GROUP)
        y = x_ref[rows, :] + 0.5 * _dot(act_ref[rows, :], wdn_ref[...])
        if final_norm:
            y = _rmsnorm(y, gf_ref[...])
        o_ref[rows, :] = y


def _ffn(x, g, wup, wdn, gf, *, tm):
    n = x.shape[0]
    final_norm = gf is not None
    in_specs = [
        pl.BlockSpec((tm, D_MODEL), lambda i: (i, 0)),
        _resident((1, D_MODEL)),
        _resident((D_MODEL, 2 * D_FF)),
        _resident((D_FF, D_MODEL)),
    ]
    args = [x, g, wup, wdn]
    if final_norm:
        in_specs.append(_resident((1, D_MODEL)))
        args.append(gf)
    return pl.pallas_call(
        functools.partial(_ffn_kernel, final_norm=final_norm),
        grid=(n // tm,),
        in_specs=in_specs,
        out_specs=pl.BlockSpec((tm, D_MODEL), lambda i: (i, 0)),
        out_shape=jax.ShapeDtypeStruct((n, D_MODEL), F32),
        scratch_shapes=[pltpu.VMEM((tm, D_MODEL), BF16), pltpu.VMEM((tm, D_FF), BF16)],
        compiler_params=pltpu.CompilerParams(
            dimension_semantics=("arbitrary",), vmem_limit_bytes=VMEM_LIMIT_BYTES),
        name="ffn_final" if final_norm else "ffn",
    )(*args)


def _mixer_kernel(sinks_ref, h_ref, gm_ref, win_ref, wau_ref, wmix_ref, pscale_ref,
                  wpu_ref, wout_ref, bias_ref, o_ref,
                  u_scr, q_scr, kT_scr, vt_scr, vb_scr, z_scr, attn_scr, pool_scr, mrg_scr, gate_scr,
                  *, tm):
    i = pl.program_id(0)
    tiles_per_seq = SEQ // tm
    blocks_per_tile = tm // BLOCK
    gate_row_groups = tm // GATE_ROWS
    assert gate_row_groups * (2 * D_MODEL // GATE_COLS) == blocks_per_tile * N_KV_HEADS
    tile_in_seq = i % tiles_per_seq
    seq_start = tile_in_seq == 0

    @pl.when(seq_start)
    def _():
        kT_scr[:, 0:BLOCK] = jnp.zeros((LANES, BLOCK), BF16)
        for scr in (vt_scr, vb_scr):
            scr[0:BLOCK, :] = jnp.zeros((BLOCK, scr.shape[1]), scr.dtype)
        z_scr[0:POOL_HIST, :] = jnp.zeros((POOL_HIST, POOL_WIDTH), F32)

    @pl.when(jnp.logical_not(seq_start))
    def _():
        kT_scr[:, 0:BLOCK] = kT_scr[:, tm:tm + BLOCK]
        for scr in (vt_scr, vb_scr):
            scr[0:BLOCK, :] = scr[tm:tm + BLOCK, :]
        z_scr[0:POOL_HIST, :] = z_scr[tm:tm + POOL_HIST, :]

    u = _rmsnorm(h_ref[...], gm_ref[...]).astype(BF16)
    u_scr[...] = u
    q_scr[...] = (_dot(u, win_ref[:, 0:_K0]) * (HEAD_DIM ** -0.5 * LOG2E)).astype(BF16)

    kv = _dot(u, win_ref[:, _K0:_Z0])
    kT_scr[:, BLOCK:] = kv[:, :LANES].T.astype(BF16)

    low = lax.broadcasted_iota(jnp.int32, (tm, LANES), 1) < HEAD_DIM
    v01 = kv[:, LANES:]
    v10 = pltpu.roll(v01, HEAD_DIM, 1)
    zero = jnp.zeros_like(v01)
    v_tops = (jnp.where(low, v01, zero), jnp.where(low, v10, zero))
    v_bots = (jnp.where(low, zero, v10), jnp.where(low, zero, v01))
    lane = lax.broadcasted_iota(jnp.int32, (tm + BLOCK, LANES), 1)
    ones_top = jnp.where(lane < HEAD_DIM, 1.0, 0.0).astype(BF16)
    ones_bot = jnp.where(lane >= HEAD_DIM, 1.0, 0.0).astype(BF16)
    for hk in range(N_KV_HEADS):
        c0 = hk * 2 * LANES
        vt_scr[BLOCK:, c0:c0 + LANES] = v_tops[hk].astype(BF16)
        vb_scr[BLOCK:, c0:c0 + LANES] = v_bots[hk].astype(BF16)
        vt_scr[:, c0 + LANES:c0 + 2 * LANES] = ones_top
        vb_scr[:, c0 + LANES:c0 + 2 * LANES] = ones_bot
    z_scr[POOL_HIST:, :] = _dot(u, win_ref[:, _Z0:_GA0])

    half_lane = lax.broadcasted_iota(jnp.int32, (BLOCK, LANES), 1) < HEAD_DIM
    kzero = jnp.zeros((HEAD_DIM, 2 * BLOCK), BF16)

    for b in range(blocks_per_tile):
        r0 = b * BLOCK
        first_block = seq_start.astype(jnp.int32) if b == 0 else 0
        for hk in range(N_KV_HEADS):
            kTb = kT_scr[hk * HEAD_DIM:(hk + 1) * HEAD_DIM, r0:r0 + 2 * BLOCK]
            k_even_odd = jnp.concatenate(
                [jnp.concatenate([kTb, kzero], axis=1), jnp.concatenate([kzero, kTb], axis=1)], axis=0)
            vt = vt_scr[r0:r0 + 2 * BLOCK, hk * 2 * LANES:(hk + 1) * 2 * LANES]
            vb = vb_scr[r0:r0 + 2 * BLOCK, hk * 2 * LANES:(hk + 1) * 2 * LANES]
            q0 = hk * Q_PER_KV * HEAD_DIM
            qs = jnp.concatenate(
                [q_scr[r0:r0 + BLOCK, q0 + p * LANES:q0 + (p + 1) * LANES] for p in range(PAIRS_PER_KV)],
                axis=0)
            s = _dot(qs, k_even_odd) + bias_ref[first_block, hk]
            s_even = s[:, :2 * BLOCK]
            s_odd = s[:, 2 * BLOCK:]
            step = b * N_KV_HEADS + hk
            grow = slice((step % gate_row_groups) * GATE_ROWS, (step % gate_row_groups + 1) * GATE_ROWS)
            gcol = (step // gate_row_groups) * GATE_COLS
            gate_scr[grow, gcol:gcol + GATE_COLS] = _dot(
                u_scr[grow, :], win_ref[:, _GA0 + gcol:_GA0 + gcol + GATE_COLS])
            e_even, e_odd, sink_terms = [], [], []
            for p in range(PAIRS_PER_KV):
                rows = slice(p * BLOCK, (p + 1) * BLOCK)
                head0 = hk * Q_PER_KV + 2 * p
                sink0 = sinks_ref[head0] * LOG2E
                sink1 = sinks_ref[head0 + 1] * LOG2E
                s0 = s_even[rows]
                s1 = s_odd[rows]
                m0 = jnp.maximum(jnp.max(s0, axis=-1, keepdims=True), sink0)
                m1 = jnp.maximum(jnp.max(s1, axis=-1, keepdims=True), sink1)
                e_even.append(jnp.exp2(s0 - m0).astype(BF16))
                e_odd.append(jnp.exp2(s1 - m1).astype(BF16))
                sink_terms.append(jnp.where(half_lane, jnp.exp2(sink0 - m0), jnp.exp2(sink1 - m1)))
            pv = (_dot(jnp.concatenate(e_even, axis=0), vt)
                  + _dot(jnp.concatenate(e_odd, axis=0), vb))
            for p in range(PAIRS_PER_KV):
                rows = slice(p * BLOCK, (p + 1) * BLOCK)
                out = pv[rows, :LANES] / (pv[rows, LANES:] + sink_terms[p])
                attn_scr[r0:r0 + BLOCK, q0 + p * LANES:q0 + (p + 1) * LANES] = out.astype(BF16)

    t = tile_in_seq * tm + lax.broadcasted_iota(jnp.int32, (tm, POOL_GROUP), 0)
    for gi, w in enumerate(POOL_WINDOWS):
        cs = slice(gi * POOL_GROUP, (gi + 1) * POOL_GROUP)
        acc = z_scr[:, cs]
        span = 1
        while span < w:
            acc = acc + pltpu.roll(acc, span, 0)
            span *= 2
        cnt = jnp.minimum(t + 1, w).astype(F32)
        pool_scr[:, cs] = (acc[POOL_HIST:] / cnt - z_scr[POOL_HIST:, cs]).astype(BF16)
    mixed = (_dot(pool_scr[...], wmix_ref[...]) * pscale_ref[...]).astype(BF16)

    half = D_MODEL // 2
    for c in range(2):
        cs = slice(c * half, (c + 1) * half)
        a = _dot(attn_scr[...], wau_ref[:, cs])
        p = _dot(mixed, wpu_ref[:, cs])
        ga = jax.nn.sigmoid(gate_scr[:, c * half:(c + 1) * half])
        gp = jax.nn.sigmoid(gate_scr[:, D_MODEL + c * half:D_MODEL + (c + 1) * half])
        mrg_scr[:, cs] = (ga * a + gp * p).astype(BF16)
    o_ref[...] = h_ref[...] + _dot(mrg_scr[...], wout_ref[...])


def _attention_bias():
    hidx = np.arange(1, N_Q_HEADS + 1, dtype=np.float32)
    slopes = (2.0 ** (-8.0 * hidx / N_Q_HEADS)).astype(np.float32)
    qi = np.arange(BLOCK)[:, None] + BLOCK
    kj = np.arange(2 * BLOCK)[None, :]
    dist = qi - kj
    valid = (dist >= 0) & (dist < WINDOW)
    valid_first = valid & (kj >= BLOCK)
    lin = -(slopes[:, None, None] * dist.astype(np.float32)[None]).astype(np.float64) * LOG2E
    tables = []
    for vis in (valid, valid_first):
        per_head = np.where(vis[None], lin, -np.inf)
        per_head = per_head.reshape(N_KV_HEADS, PAIRS_PER_KV, 2, BLOCK, 2 * BLOCK)
        stacked = per_head.transpose(0, 1, 3, 2, 4).reshape(N_KV_HEADS, PAIRS_PER_KV * BLOCK, 4 * BLOCK)
        tables.append(stacked)
    return jnp.asarray(np.stack(tables).astype(np.float32))


def _mixer(h, sinks, gm, win, wau, wmix, pscale, wpu, wout, bias, *, tm):
    n = h.shape[0]
    kvw = N_KV_HEADS * LANES
    return pl.pallas_call(
        functools.partial(_mixer_kernel, tm=tm),
        grid=(n // tm,),
        in_specs=[
            pl.BlockSpec(memory_space=pltpu.SMEM),
            pl.BlockSpec((tm, D_MODEL), lambda i: (i, 0)),
            _resident((1, D_MODEL)),
            _resident((D_MODEL, IN_WIDTH)),
            _resident((ATTN_WIDTH, D_MODEL)),
            _resident((POOL_WIDTH, POOL_WIDTH)),
            _resident((1, POOL_WIDTH)),
            _resident((POOL_WIDTH, D_MODEL)),
            _resident((D_MODEL, D_MODEL)),
            _resident((2, N_KV_HEADS, PAIRS_PER_KV * BLOCK, 4 * BLOCK)),
        ],
        out_specs=pl.BlockSpec((tm, D_MODEL), lambda i: (i, 0)),
        out_shape=jax.ShapeDtypeStruct((n, D_MODEL), F32),
        scratch_shapes=[
            pltpu.VMEM((tm, D_MODEL), BF16),
            pltpu.VMEM((tm, ATTN_WIDTH), BF16),
            pltpu.VMEM((LANES, tm + BLOCK), BF16),
            pltpu.VMEM((tm + BLOCK, 2 * kvw), BF16),
            pltpu.VMEM((tm + BLOCK, 2 * kvw), BF16),
            pltpu.VMEM((tm + POOL_HIST, POOL_WIDTH), F32),
            pltpu.VMEM((tm, ATTN_WIDTH), BF16),
            pltpu.VMEM((tm, POOL_WIDTH), BF16),
            pltpu.VMEM((tm, D_MODEL), BF16),
            pltpu.VMEM((tm, 2 * D_MODEL), F32),
        ],
        compiler_params=pltpu.CompilerParams(
            dimension_semantics=("arbitrary",), vmem_limit_bytes=VMEM_LIMIT_BYTES),
        name="mixer",
    )(sinks, h, gm, win, wau, wmix, pscale, wpu, wout, bias)


def _block_diagonal(blocks):
    g, n, _ = blocks.shape
    out = jnp.zeros((g * n, g * n), blocks.dtype)
    for gi in range(g):
        out = out.at[gi * n:(gi + 1) * n, gi * n:(gi + 1) * n].set(blocks[gi])
    return out


def kernel(x, ffn1_norm, ffn1_w_up, ffn1_w_down, mix_norm, w_in, sinks, w_attn_up, pool_w_mix, pool_scale,
           w_pool_up, w_out, ffn2_norm, ffn2_w_up, ffn2_w_down, final_norm):
    batch, seq, d = x.shape
    depth = ffn1_norm.shape[0]
    assert (seq, d) == (SEQ, D_MODEL)
    h = x.reshape(batch * seq, d)
    bias = _attention_bias()
    for l in range(depth):
        h = _ffn(h, ffn1_norm[l].reshape(1, d), ffn1_w_up[l].astype(BF16), ffn1_w_down[l].astype(BF16), None,
                 tm=FFN_ROWS)
        h = _mixer(h, sinks[l], mix_norm[l].reshape(1, d), w_in[l].astype(BF16), w_attn_up[l].astype(BF16),
                   _block_diagonal(pool_w_mix[l]).astype(BF16), pool_scale[l].reshape(1, POOL_WIDTH),
                   w_pool_up[l].astype(BF16), w_out[l].astype(BF16), bias, tm=MIXER_ROWS)
        last = l == depth - 1
        h = _ffn(h, ffn2_norm[l].reshape(1, d), ffn2_w_up[l].astype(BF16), ffn2_w_down[l].astype(BF16),
                 final_norm.reshape(1, d) if last else None, tm=FFN_ROWS)
    return h.reshape(batch, seq, d)
```

```python
import functools
import math

import jax
import jax.numpy as jnp
import numpy as np
from jax import lax
from jax.experimental import pallas as pl
from jax.experimental.pallas import tpu as pltpu

D_MODEL = 1024
SEQ = 8192
N_Q_HEADS = 16
N_KV_HEADS = 2
HEAD_DIM = 64
Q_PER_KV = N_Q_HEADS // N_KV_HEADS
WINDOW = 128
BLOCK = 128
ATTN_WIDTH = N_Q_HEADS * HEAD_DIM
KV_WIDTH = N_KV_HEADS * HEAD_DIM
POOL_WINDOWS = (2, 4, 8, 16)
POOL_WIDTH = 512
POOL_GROUP = POOL_WIDTH // len(POOL_WINDOWS)
D_FF = 2816
NORM_EPS = 1e-6

LANES = 128
MXU_DIM = 256
BF16_TILE_ROWS = 16
VMEM_LIMIT_BYTES = 60 * 1024 * 1024

FFN_ROWS = 1024
MIXER_ROWS = 1024
ROW_GROUP = MXU_DIM
FF_CHUNK = MXU_DIM
N_FF_CHUNKS = D_FF // FF_CHUNK
POOL_HIST = 16
PAIRS_PER_KV = Q_PER_KV // 2
GATE_ROWS = MXU_DIM
GATE_COLS = 2 * MXU_DIM
LOG2E = math.log2(math.e)

_K0 = ATTN_WIDTH
_V0 = _K0 + KV_WIDTH
_Z0 = _V0 + KV_WIDTH
_GA0 = _Z0 + POOL_WIDTH
_GP0 = _GA0 + D_MODEL
IN_WIDTH = _GP0 + D_MODEL

F32 = jnp.float32
BF16 = jnp.bfloat16


def _resident(shape):
    nd = len(shape)
    return pl.BlockSpec(shape, lambda i: (0,) * nd, pipeline_mode=pl.Buffered(1))


def _rmsnorm(x, g):
    return x * lax.rsqrt(jnp.mean(x * x, axis=-1, keepdims=True) + NORM_EPS) * g


def _dot(a, b):
    return jnp.dot(a, b, preferred_element_type=F32)


def _dot_nt(a, b):
    return lax.dot_general(a, b, (((1,), (1,)), ((), ())), preferred_element_type=F32)


def _passenger_spec(shape, steps):
    rows, cols = shape
    assert rows % BF16_TILE_ROWS == 0
    nblk = steps
    while rows % nblk or (rows // nblk) % BF16_TILE_ROWS:
        nblk //= 2
    per = steps // nblk
    return pl.BlockSpec((rows // nblk, cols), lambda i: (i // per, 0))


def _cast_passengers(src_refs, dst_refs):
    for src, dst in zip(src_refs, dst_refs):
        dst[...] = src[...].astype(BF16)


def _ffn_kernel(x_ref, g_ref, wup_ref, wdn_ref, *rest, final_norm, n_cast):
    rest = list(rest)
    gf_ref = rest.pop(0) if final_norm else None
    cast_in, (o_ref, *cast_out), (xn_ref, act_ref) = rest[:n_cast], rest[n_cast:2 * n_cast + 1], rest[-2:]
    _cast_passengers(cast_in, cast_out)
    tm = x_ref.shape[0]

    def swiglu_chunk(c, rows):
        cs = slice(c * FF_CHUNK, (c + 1) * FF_CHUNK)
        a = _dot(xn_ref[rows, :], wup_ref[:, cs])
        b = _dot(xn_ref[rows, :], wup_ref[:, D_FF + c * FF_CHUNK:D_FF + (c + 1) * FF_CHUNK])
        act_ref[rows, cs] = (a * jax.nn.sigmoid(a) * b).astype(BF16)

    for r in range(tm // ROW_GROUP):
        rows = slice(r * ROW_GROUP, (r + 1) * ROW_GROUP)
        xn_ref[rows, :] = _rmsnorm(x_ref[rows, :], g_ref[...]).astype(BF16)
        swiglu_chunk(0, rows)
    for c in range(1, N_FF_CHUNKS):
        swiglu_chunk(c, slice(None))
    for r in range(tm // ROW_GROUP):
        rows = slice(r * ROW_GROUP, (r + 1) * ROW_GROUP)
        y = x_ref[rows, :] + 0.5 * _dot(act_ref[rows, :], wdn_ref[...])
        if final_norm:
            y = _rmsnorm(y, gf_ref[...])
        o_ref[rows, :] = y


def _ffn(x, g, wup, wdn, gf, passengers, *, tm):
    n = x.shape[0]
    steps = n // tm
    final_norm = gf is not None
    in_specs = [
        pl.BlockSpec((tm, D_MODEL), lambda i: (i, 0)),
        _resident((1, D_MODEL)),
        _resident((D_MODEL, 2 * D_FF)),
        _resident((D_FF, D_MODEL)),
    ]
    args = [x, g, wup, wdn]
    if final_norm:
        in_specs.append(_resident((1, D_MODEL)))
        args.append(gf)
    cast_specs = [_passenger_spec(w.shape, steps) for w in passengers]
    out = pl.pallas_call(
        functools.partial(_ffn_kernel, final_norm=final_norm, n_cast=len(passengers)),
        grid=(steps,),
        in_specs=in_specs + cast_specs,
        out_specs=[pl.BlockSpec((tm, D_MODEL), lambda i: (i, 0))] + cast_specs,
        out_shape=[jax.ShapeDtypeStruct((n, D_MODEL), F32)]
        + [jax.ShapeDtypeStruct(w.shape, BF16) for w in passengers],
        scratch_shapes=[pltpu.VMEM((tm, D_MODEL), BF16), pltpu.VMEM((tm, D_FF), BF16)],
        compiler_params=pltpu.CompilerParams(
            dimension_semantics=("arbitrary",), vmem_limit_bytes=VMEM_LIMIT_BYTES),
        name="ffn_final" if final_norm else "ffn",
    )(*args, *passengers)
    return out[0], out[1:]


def _mixer_kernel(sinks_ref, h_ref, gm_ref, win_ref, wau_ref, wmix_ref, pscale_ref,
                  wpu_ref, wout_ref, bias_ref, *rest, tm, n_cast):
    cast_in, (o_ref, *cast_out) = rest[:n_cast], rest[n_cast:2 * n_cast + 1]
    (u_scr, q_scr, kT_scr, vt_scr, vb_scr, z_scr, attn_scr, pool_scr, mrg_scr, gate_scr) = rest[2 * n_cast + 1:]
    _cast_passengers(cast_in, cast_out)
    i = pl.program_id(0)
    tiles_per_seq = SEQ // tm
    blocks_per_tile = tm // BLOCK
    gate_row_groups = tm // GATE_ROWS
    assert gate_row_groups * (2 * D_MODEL // GATE_COLS) == blocks_per_tile * N_KV_HEADS
    tile_in_seq = i % tiles_per_seq
    seq_start = tile_in_seq == 0

    @pl.when(seq_start)
    def _():
        kT_scr[:, 0:BLOCK] = jnp.zeros((LANES, BLOCK), BF16)
        for scr in (vt_scr, vb_scr):
            scr[0:BLOCK, :] = jnp.zeros((BLOCK, scr.shape[1]), scr.dtype)
        z_scr[0:POOL_HIST, :] = jnp.zeros((POOL_HIST, POOL_WIDTH), F32)

    @pl.when(jnp.logical_not(seq_start))
    def _():
        kT_scr[:, 0:BLOCK] = kT_scr[:, tm:tm + BLOCK]
        for scr in (vt_scr, vb_scr):
            scr[0:BLOCK, :] = scr[tm:tm + BLOCK, :]
        z_scr[0:POOL_HIST, :] = z_scr[tm:tm + POOL_HIST, :]

    u = _rmsnorm(h_ref[...], gm_ref[...]).astype(BF16)
    u_scr[...] = u
    q_scr[...] = (_dot(u, win_ref[:, 0:_K0]) * (HEAD_DIM ** -0.5 * LOG2E)).astype(BF16)

    kv = _dot(u, win_ref[:, _K0:_Z0])
    kT_scr[:, BLOCK:] = kv[:, :LANES].T.astype(BF16)

    low = lax.broadcasted_iota(jnp.int32, (tm, LANES), 1) < HEAD_DIM
    v01 = kv[:, LANES:]
    v10 = pltpu.roll(v01, HEAD_DIM, 1)
    zero = jnp.zeros_like(v01)
    v_tops = (jnp.where(low, v01, zero), jnp.where(low, v10, zero))
    v_bots = (jnp.where(low, zero, v10), jnp.where(low, zero, v01))
    lane = lax.broadcasted_iota(jnp.int32, (tm + BLOCK, LANES), 1)
    ones_top = jnp.where(lane < HEAD_DIM, 1.0, 0.0).astype(BF16)
    ones_bot = jnp.where(lane >= HEAD_DIM, 1.0, 0.0).astype(BF16)
    for hk in range(N_KV_HEADS):
        c0 = hk * 2 * LANES
        vt_scr[BLOCK:, c0:c0 + LANES] = v_tops[hk].astype(BF16)
        vb_scr[BLOCK:, c0:c0 + LANES] = v_bots[hk].astype(BF16)
        vt_scr[:, c0 + LANES:c0 + 2 * LANES] = ones_top
        vb_scr[:, c0 + LANES:c0 + 2 * LANES] = ones_bot
    z_scr[POOL_HIST:, :] = _dot(u, win_ref[:, _Z0:_GA0])

    half_lane = lax.broadcasted_iota(jnp.int32, (BLOCK, LANES), 1) < HEAD_DIM
    kzero = jnp.zeros((HEAD_DIM, 2 * BLOCK), BF16)

    for b in range(blocks_per_tile):
        r0 = b * BLOCK
        first_block = seq_start.astype(jnp.int32) if b == 0 else 0
        for hk in range(N_KV_HEADS):
            kTb = kT_scr[hk * HEAD_DIM:(hk + 1) * HEAD_DIM, r0:r0 + 2 * BLOCK]
            k_even_odd = jnp.concatenate(
                [jnp.concatenate([kTb, kzero], axis=1), jnp.concatenate([kzero, kTb], axis=1)], axis=0)
            vt = vt_scr[r0:r0 + 2 * BLOCK, hk * 2 * LANES:(hk + 1) * 2 * LANES]
            vb = vb_scr[r0:r0 + 2 * BLOCK, hk * 2 * LANES:(hk + 1) * 2 * LANES]
            q0 = hk * Q_PER_KV * HEAD_DIM
            qs = jnp.concatenate(
                [q_scr[r0:r0 + BLOCK, q0 + p * LANES:q0 + (p + 1) * LANES] for p in range(PAIRS_PER_KV)],
                axis=0)
            s = _dot(qs, k_even_odd) + bias_ref[first_block, hk]
            s_even = s[:, :2 * BLOCK]
            s_odd = s[:, 2 * BLOCK:]
            step = b * N_KV_HEADS + hk
            grow = slice((step % gate_row_groups) * GATE_ROWS, (step % gate_row_groups + 1) * GATE_ROWS)
            gcol = (step // gate_row_groups) * GATE_COLS
            gate_scr[grow, gcol:gcol + GATE_COLS] = _dot(
                u_scr[grow, :], win_ref[:, _GA0 + gcol:_GA0 + gcol + GATE_COLS])
            e_even, e_odd, sink_terms = [], [], []
            for p in range(PAIRS_PER_KV):
                rows = slice(p * BLOCK, (p + 1) * BLOCK)
                head0 = hk * Q_PER_KV + 2 * p
                sink0 = sinks_ref[head0] * LOG2E
                sink1 = sinks_ref[head0 + 1] * LOG2E
                s0 = s_even[rows]
                s1 = s_odd[rows]
                m0 = jnp.maximum(jnp.max(s0, axis=-1, keepdims=True), sink0)
                m1 = jnp.maximum(jnp.max(s1, axis=-1, keepdims=True), sink1)
                e_even.append(jnp.exp2(s0 - m0).astype(BF16))
                e_odd.append(jnp.exp2(s1 - m1).astype(BF16))
                sink_terms.append(jnp.where(half_lane, jnp.exp2(sink0 - m0), jnp.exp2(sink1 - m1)))
            pv = (_dot(jnp.concatenate(e_even, axis=0), vt)
                  + _dot(jnp.concatenate(e_odd, axis=0), vb))
            for p in range(PAIRS_PER_KV):
                rows = slice(p * BLOCK, (p + 1) * BLOCK)
                out = pv[rows, :LANES] / (pv[rows, LANES:] + sink_terms[p])
                attn_scr[r0:r0 + BLOCK, q0 + p * LANES:q0 + (p + 1) * LANES] = out.astype(BF16)

    t = tile_in_seq * tm + lax.broadcasted_iota(jnp.int32, (tm, POOL_GROUP), 0)
    for gi, w in enumerate(POOL_WINDOWS):
        cs = slice(gi * POOL_GROUP, (gi + 1) * POOL_GROUP)
        acc = z_scr[:, cs]
        span = 1
        while span < w:
            acc = acc + pltpu.roll(acc, span, 0)
            span *= 2
        cnt = jnp.minimum(t + 1, w).astype(F32)
        pool_scr[:, cs] = (acc[POOL_HIST:] / cnt - z_scr[POOL_HIST:, cs]).astype(BF16)
    mixed = (_dot(pool_scr[...], wmix_ref[...]) * pscale_ref[...]).astype(BF16)

    half = D_MODEL // 2
    for c in range(2):
        cs = slice(c * half, (c + 1) * half)
        a = _dot(attn_scr[...], wau_ref[:, cs])
        p = _dot(mixed, wpu_ref[:, cs])
        ga = jax.nn.sigmoid(gate_scr[:, c * half:(c + 1) * half])
        gp = jax.nn.sigmoid(gate_scr[:, D_MODEL + c * half:D_MODEL + (c + 1) * half])
        mrg_scr[:, cs] = (ga * a + gp * p).astype(BF16)
    o_ref[...] = h_ref[...] + _dot(mrg_scr[...], wout_ref[...])


def _attention_bias():
    hidx = np.arange(1, N_Q_HEADS + 1, dtype=np.float32)
    slopes = (2.0 ** (-8.0 * hidx / N_Q_HEADS)).astype(np.float32)
    qi = np.arange(BLOCK)[:, None] + BLOCK
    kj = np.arange(2 * BLOCK)[None, :]
    dist = qi - kj
    valid = (dist >= 0) & (dist < WINDOW)
    valid_first = valid & (kj >= BLOCK)
    lin = -(slopes[:, None, None] * dist.astype(np.float32)[None]).astype(np.float64) * LOG2E
    tables = []
    for vis in (valid, valid_first):
        per_head = np.where(vis[None], lin, -np.inf)
        per_head = per_head.reshape(N_KV_HEADS, PAIRS_PER_KV, 2, BLOCK, 2 * BLOCK)
        stacked = per_head.transpose(0, 1, 3, 2, 4).reshape(N_KV_HEADS, PAIRS_PER_KV * BLOCK, 4 * BLOCK)
        tables.append(stacked)
    return jnp.asarray(np.stack(tables).astype(np.float32))


def _mixer(h, sinks, gm, win, wau, wmix, pscale, wpu, wout, bias, passengers, *, tm):
    n = h.shape[0]
    steps = n // tm
    kvw = N_KV_HEADS * LANES
    cast_specs = [_passenger_spec(w.shape, steps) for w in passengers]
    out = pl.pallas_call(
        functools.partial(_mixer_kernel, tm=tm, n_cast=len(passengers)),
        grid=(steps,),
        in_specs=[
            pl.BlockSpec(memory_space=pltpu.SMEM),
            pl.BlockSpec((tm, D_MODEL), lambda i: (i, 0)),
            _resident((1, D_MODEL)),
            _resident((D_MODEL, IN_WIDTH)),
            _resident((ATTN_WIDTH, D_MODEL)),
            _resident((POOL_WIDTH, POOL_WIDTH)),
            _resident((1, POOL_WIDTH)),
            _resident((POOL_WIDTH, D_MODEL)),
            _resident((D_MODEL, D_MODEL)),
            _resident((2, N_KV_HEADS, PAIRS_PER_KV * BLOCK, 4 * BLOCK)),
        ] + cast_specs,
        out_specs=[pl.BlockSpec((tm, D_MODEL), lambda i: (i, 0))] + cast_specs,
        out_shape=[jax.ShapeDtypeStruct((n, D_MODEL), F32)]
        + [jax.ShapeDtypeStruct(w.shape, BF16) for w in passengers],
        scratch_shapes=[
            pltpu.VMEM((tm, D_MODEL), BF16),
            pltpu.VMEM((tm, ATTN_WIDTH), BF16),
            pltpu.VMEM((LANES, tm + BLOCK), BF16),
            pltpu.VMEM((tm + BLOCK, 2 * kvw), BF16),
            pltpu.VMEM((tm + BLOCK, 2 * kvw), BF16),
            pltpu.VMEM((tm + POOL_HIST, POOL_WIDTH), F32),
            pltpu.VMEM((tm, ATTN_WIDTH), BF16),
            pltpu.VMEM((tm, POOL_WIDTH), BF16),
            pltpu.VMEM((tm, D_MODEL), BF16),
            pltpu.VMEM((tm, 2 * D_MODEL), F32),
        ],
        compiler_params=pltpu.CompilerParams(
            dimension_semantics=("arbitrary",), vmem_limit_bytes=VMEM_LIMIT_BYTES),
        name="mixer",
    )(sinks, h, gm, win, wau, wmix, pscale, wpu, wout, bias, *passengers)
    return out[0], out[1:]


def _block_diagonal(blocks):
    g, n, _ = blocks.shape
    out = jnp.zeros((g * n, g * n), blocks.dtype)
    for gi in range(g):
        out = out.at[gi * n:(gi + 1) * n, gi * n:(gi + 1) * n].set(blocks[gi])
    return out


def kernel(x, ffn1_norm, ffn1_w_up, ffn1_w_down, mix_norm, w_in, sinks, w_attn_up, pool_w_mix, pool_scale,
           w_pool_up, w_out, ffn2_norm, ffn2_w_up, ffn2_w_down, final_norm):
    batch, seq, d = x.shape
    depth = ffn1_norm.shape[0]
    assert (seq, d) == (SEQ, D_MODEL)
    h = x.reshape(batch * seq, d)
    bias = _attention_bias()
    ffn1_w = (ffn1_w_up[0].astype(BF16), ffn1_w_down[0].astype(BF16))
    for l in range(depth):
        last = l == depth - 1
        h, (win, wau, wmix, wpu, wout) = _ffn(
            h, ffn1_norm[l].reshape(1, d), *ffn1_w, None,
            [w_in[l], w_attn_up[l], _block_diagonal(pool_w_mix[l]), w_pool_up[l], w_out[l]], tm=FFN_ROWS)
        h, ffn2_w = _mixer(h, sinks[l], mix_norm[l].reshape(1, d), win, wau, wmix,
                           pool_scale[l].reshape(1, POOL_WIDTH), wpu, wout, bias,
                           [ffn2_w_up[l], ffn2_w_down[l]], tm=MIXER_ROWS)
        h, ffn1_w = _ffn(h, ffn2_norm[l].reshape(1, d), *ffn2_w, final_norm.reshape(1, d) if last else None,
                         [] if last else [ffn1_w_up[l + 1], ffn1_w_down[l + 1]], tm=FFN_ROWS)
    return h.reshape(batch, seq, d)
```

```python
import functools
import math

import jax
import jax.numpy as jnp
import numpy as np
from jax import lax
from jax.experimental import pallas as pl
from jax.experimental.pallas import tpu as pltpu

D_MODEL = 1024
SEQ = 8192
N_Q_HEADS = 16
N_KV_HEADS = 2
HEAD_DIM = 64
Q_PER_KV = N_Q_HEADS // N_KV_HEADS
WINDOW = 128
BLOCK = 128
ATTN_WIDTH = N_Q_HEADS * HEAD_DIM
KV_WIDTH = N_KV_HEADS * HEAD_DIM
POOL_WINDOWS = (2, 4, 8, 16)
POOL_WIDTH = 512
POOL_GROUP = POOL_WIDTH // len(POOL_WINDOWS)
D_FF = 2816
NORM_EPS = 1e-6

LANES = 128
MXU_DIM = 256
BF16_TILE_ROWS = 16
VMEM_LIMIT_BYTES = 60 * 1024 * 1024

FFN_ROWS = 1024
MIXER_ROWS = 1024
ROW_GROUP = MXU_DIM
FF_CHUNK = MXU_DIM
N_FF_CHUNKS = D_FF // FF_CHUNK
POOL_HIST = 16
PAIRS_PER_KV = Q_PER_KV // 2
GATE_ROWS = MXU_DIM
GATE_COLS = 2 * MXU_DIM
LOG2E = math.log2(math.e)

_K0 = ATTN_WIDTH
_V0 = _K0 + KV_WIDTH
_Z0 = _V0 + KV_WIDTH
_GA0 = _Z0 + POOL_WIDTH
_GP0 = _GA0 + D_MODEL
IN_WIDTH = _GP0 + D_MODEL

F32 = jnp.float32
BF16 = jnp.bfloat16


def _resident(shape):
    nd = len(shape)
    return pl.BlockSpec(shape, lambda i: (0,) * nd, pipeline_mode=pl.Buffered(1))


def _rmsnorm(x, g):
    return x * lax.rsqrt(jnp.mean(x * x, axis=-1, keepdims=True) + NORM_EPS) * g


def _dot(a, b):
    return jnp.dot(a, b, preferred_element_type=F32)


def _dot_nt(a, b):
    return lax.dot_general(a, b, (((1,), (1,)), ((), ())), preferred_element_type=F32)


def _passenger_spec(shape, steps):
    rows, cols = shape
    assert rows % BF16_TILE_ROWS == 0
    nblk = steps
    while rows % nblk or (rows // nblk) % BF16_TILE_ROWS:
        nblk //= 2
    per = steps // nblk
    return pl.BlockSpec((rows // nblk, cols), lambda i: (i // per, 0))


def _cast_passengers(src_refs, dst_refs):
    for src, dst in zip(src_refs, dst_refs):
        dst[...] = src[...].astype(BF16)


def _ffn_kernel(x_ref, g_ref, wup_ref, wdn_ref, *rest, final_norm, n_cast):
    rest = list(rest)
    gf_ref = rest.pop(0) if final_norm else None
    cast_in, (o_ref, *cast_out), (xn_ref, act_ref) = rest[:n_cast], rest[n_cast:2 * n_cast + 1], rest[-2:]
    tm = x_ref.shape[0]

    def swiglu_chunk(c, rows):
        cs = slice(c * FF_CHUNK, (c + 1) * FF_CHUNK)
        a = _dot(xn_ref[rows, :], wup_ref[:, cs])
        b = _dot(xn_ref[rows, :], wup_ref[:, D_FF + c * FF_CHUNK:D_FF + (c + 1) * FF_CHUNK])
        act_ref[rows, cs] = (a * jax.nn.sigmoid(a) * b).astype(BF16)

    for r in range(tm // ROW_GROUP):
        rows = slice(r * ROW_GROUP, (r + 1) * ROW_GROUP)
        xn_ref[rows, :] = _rmsnorm(x_ref[rows, :], g_ref[...]).astype(BF16)
        swiglu_chunk(0, rows)
    for c in range(1, N_FF_CHUNKS):
        swiglu_chunk(c, slice(None))
    _cast_passengers(cast_in, cast_out)
    for r in range(tm // ROW_GROUP):
        rows = slice(r * ROW_GROUP, (r + 1) * ROW_GROUP)
        y = x_ref[rows, :] + 0.5 * _dot(act_ref[rows, :], wdn_ref[...])
        if final_norm:
            y = _rmsnorm(y, gf_ref[...])
        o_ref[rows, :] = y


def _ffn(x, g, wup, wdn, gf, passengers, *, tm):
    n = x.shape[0]
    steps = n // tm
    final_norm = gf is not None
    in_specs = [
        pl.BlockSpec((tm, D_MODEL), lambda i: (i, 0)),
        _resident((1, D_MODEL)),
        _resident((D_MODEL, 2 * D_FF)),
        _resident((D_FF, D_MODEL)),
    ]
    args = [x, g, wup, wdn]
    if final_norm:
        in_specs.append(_resident((1, D_MODEL)))
        args.append(gf)
    cast_specs = [_passenger_spec(w.shape, steps) for w in passengers]
    out = pl.pallas_call(
        functools.partial(_ffn_kernel, final_norm=final_norm, n_cast=len(passengers)),
        grid=(steps,),
        in_specs=in_specs + cast_specs,
        out_specs=[pl.BlockSpec((tm, D_MODEL), lambda i: (i, 0))] + cast_specs,
        out_shape=[jax.ShapeDtypeStruct((n, D_MODEL), F32)]
        + [jax.ShapeDtypeStruct(w.shape, BF16) for w in passengers],
        scratch_shapes=[pltpu.VMEM((tm, D_MODEL), BF16), pltpu.VMEM((tm, D_FF), BF16)],
        compiler_params=pltpu.CompilerParams(
            dimension_semantics=("arbitrary",), vmem_limit_bytes=VMEM_LIMIT_BYTES),
        name="ffn_final" if final_norm else "ffn",
    )(*args, *passengers)
    return out[0], out[1:]


def _mixer_kernel(sinks_ref, h_ref, gm_ref, win_ref, wau_ref, wmix_ref, pscale_ref,
                  wpu_ref, wout_ref, bias_ref, *rest, tm, n_cast):
    cast_in, (o_ref, *cast_out) = rest[:n_cast], rest[n_cast:2 * n_cast + 1]
    (u_scr, q_scr, kT_scr, vt_scr, vb_scr, z_scr, attn_scr, pool_scr, mrg_scr, gate_scr) = rest[2 * n_cast + 1:]
    i = pl.program_id(0)
    tiles_per_seq = SEQ // tm
    blocks_per_tile = tm // BLOCK
    gate_row_groups = tm // GATE_ROWS
    assert gate_row_groups * (2 * D_MODEL // GATE_COLS) == blocks_per_tile * N_KV_HEADS
    tile_in_seq = i % tiles_per_seq
    seq_start = tile_in_seq == 0

    @pl.when(seq_start)
    def _():
        kT_scr[:, 0:BLOCK] = jnp.zeros((LANES, BLOCK), BF16)
        for scr in (vt_scr, vb_scr):
            scr[0:BLOCK, :] = jnp.zeros((BLOCK, scr.shape[1]), scr.dtype)
        z_scr[0:POOL_HIST, :] = jnp.zeros((POOL_HIST, POOL_WIDTH), F32)

    @pl.when(jnp.logical_not(seq_start))
    def _():
        kT_scr[:, 0:BLOCK] = kT_scr[:, tm:tm + BLOCK]
        for scr in (vt_scr, vb_scr):
            scr[0:BLOCK, :] = scr[tm:tm + BLOCK, :]
        z_scr[0:POOL_HIST, :] = z_scr[tm:tm + POOL_HIST, :]

    for r in range(tm // ROW_GROUP):
        rows = slice(r * ROW_GROUP, (r + 1) * ROW_GROUP)
        u_r = _rmsnorm(h_ref[rows, :], gm_ref[...]).astype(BF16)
        u_scr[rows, :] = u_r
        q_scr[rows, :] = (_dot(u_r, win_ref[:, 0:_K0]) * (HEAD_DIM ** -0.5 * LOG2E)).astype(BF16)
    u = u_scr[...]

    kv = _dot(u, win_ref[:, _K0:_Z0])
    kT_scr[:, BLOCK:] = kv[:, :LANES].T.astype(BF16)

    low = lax.broadcasted_iota(jnp.int32, (tm, LANES), 1) < HEAD_DIM
    v01 = kv[:, LANES:]
    v10 = pltpu.roll(v01, HEAD_DIM, 1)
    zero = jnp.zeros_like(v01)
    v_tops = (jnp.where(low, v01, zero), jnp.where(low, v10, zero))
    v_bots = (jnp.where(low, zero, v10), jnp.where(low, zero, v01))
    lane = lax.broadcasted_iota(jnp.int32, (tm + BLOCK, LANES), 1)
    ones_top = jnp.where(lane < HEAD_DIM, 1.0, 0.0).astype(BF16)
    ones_bot = jnp.where(lane >= HEAD_DIM, 1.0, 0.0).astype(BF16)
    for hk in range(N_KV_HEADS):
        c0 = hk * 2 * LANES
        vt_scr[BLOCK:, c0:c0 + LANES] = v_tops[hk].astype(BF16)
        vb_scr[BLOCK:, c0:c0 + LANES] = v_bots[hk].astype(BF16)
        vt_scr[:, c0 + LANES:c0 + 2 * LANES] = ones_top
        vb_scr[:, c0 + LANES:c0 + 2 * LANES] = ones_bot
    z_scr[POOL_HIST:, :] = _dot(u, win_ref[:, _Z0:_GA0])

    half_lane = lax.broadcasted_iota(jnp.int32, (BLOCK, LANES), 1) < HEAD_DIM
    kzero = jnp.zeros((HEAD_DIM, 2 * BLOCK), BF16)

    for b in range(blocks_per_tile):
        r0 = b * BLOCK
        first_block = seq_start.astype(jnp.int32) if b == 0 else 0
        for hk in range(N_KV_HEADS):
            kTb = kT_scr[hk * HEAD_DIM:(hk + 1) * HEAD_DIM, r0:r0 + 2 * BLOCK]
            k_even_odd = jnp.concatenate(
                [jnp.concatenate([kTb, kzero], axis=1), jnp.concatenate([kzero, kTb], axis=1)], axis=0)
            vt = vt_scr[r0:r0 + 2 * BLOCK, hk * 2 * LANES:(hk + 1) * 2 * LANES]
            vb = vb_scr[r0:r0 + 2 * BLOCK, hk * 2 * LANES:(hk + 1) * 2 * LANES]
            q0 = hk * Q_PER_KV * HEAD_DIM
            qs = jnp.concatenate(
                [q_scr[r0:r0 + BLOCK, q0 + p * LANES:q0 + (p + 1) * LANES] for p in range(PAIRS_PER_KV)],
                axis=0)
            s = _dot(qs, k_even_odd) + bias_ref[first_block, hk]
            s_even = s[:, :2 * BLOCK]
            s_odd = s[:, 2 * BLOCK:]
            step = b * N_KV_HEADS + hk
            grow = slice((step % gate_row_groups) * GATE_ROWS, (step % gate_row_groups + 1) * GATE_ROWS)
            gcol = (step // gate_row_groups) * GATE_COLS
            gate_scr[grow, gcol:gcol + GATE_COLS] = _dot(
                u_scr[grow, :], win_ref[:, _GA0 + gcol:_GA0 + gcol + GATE_COLS])
            e_even, e_odd, sink_terms = [], [], []
            for p in range(PAIRS_PER_KV):
                rows = slice(p * BLOCK, (p + 1) * BLOCK)
                head0 = hk * Q_PER_KV + 2 * p
                sink0 = sinks_ref[head0] * LOG2E
                sink1 = sinks_ref[head0 + 1] * LOG2E
                s0 = s_even[rows]
                s1 = s_odd[rows]
                m0 = jnp.maximum(jnp.max(s0, axis=-1, keepdims=True), sink0)
                m1 = jnp.maximum(jnp.max(s1, axis=-1, keepdims=True), sink1)
                e_even.append(jnp.exp2(s0 - m0).astype(BF16))
                e_odd.append(jnp.exp2(s1 - m1).astype(BF16))
                sink_terms.append(jnp.exp2(jnp.where(half_lane, sink0 - m0, sink1 - m1)))
            pv = (_dot(jnp.concatenate(e_even, axis=0), vt)
                  + _dot(jnp.concatenate(e_odd, axis=0), vb))
            for p in range(PAIRS_PER_KV):
                rows = slice(p * BLOCK, (p + 1) * BLOCK)
                out = pv[rows, :LANES] / (pv[rows, LANES:] + sink_terms[p])
                attn_scr[r0:r0 + BLOCK, q0 + p * LANES:q0 + (p + 1) * LANES] = out.astype(BF16)

    t = tile_in_seq * tm + lax.broadcasted_iota(jnp.int32, (tm, POOL_GROUP), 0)
    for gi, w in enumerate(POOL_WINDOWS):
        cs = slice(gi * POOL_GROUP, (gi + 1) * POOL_GROUP)
        acc = z_scr[:, cs]
        span = 1
        while span < w:
            acc = acc + pltpu.roll(acc, span, 0)
            span *= 2
        cnt = jnp.minimum(t + 1, w).astype(F32)
        pool_scr[:, cs] = (acc[POOL_HIST:] / cnt - z_scr[POOL_HIST:, cs]).astype(BF16)
    mixed = (_dot(pool_scr[...], wmix_ref[...]) * pscale_ref[...]).astype(BF16)

    half = D_MODEL // 2
    for c in range(2):
        cs = slice(c * half, (c + 1) * half)
        a = _dot(attn_scr[...], wau_ref[:, cs])
        p = _dot(mixed, wpu_ref[:, cs])
        ga = jax.nn.sigmoid(gate_scr[:, c * half:(c + 1) * half])
        gp = jax.nn.sigmoid(gate_scr[:, D_MODEL + c * half:D_MODEL + (c + 1) * half])
        mrg_scr[:, cs] = (ga * a + gp * p).astype(BF16)
    _cast_passengers(cast_in, cast_out)
    for r in range(tm // ROW_GROUP):
        rows = slice(r * ROW_GROUP, (r + 1) * ROW_GROUP)
        o_ref[rows, :] = h_ref[rows, :] + _dot(mrg_scr[rows, :], wout_ref[...])


def _attention_bias():
    hidx = np.arange(1, N_Q_HEADS + 1, dtype=np.float32)
    slopes = (2.0 ** (-8.0 * hidx / N_Q_HEADS)).astype(np.float32)
    qi = np.arange(BLOCK)[:, None] + BLOCK
    kj = np.arange(2 * BLOCK)[None, :]
    dist = qi - kj
    valid = (dist >= 0) & (dist < WINDOW)
    valid_first = valid & (kj >= BLOCK)
    lin = -(slopes[:, None, None] * dist.astype(np.float32)[None]).astype(np.float64) * LOG2E
    tables = []
    for vis in (valid, valid_first):
        per_head = np.where(vis[None], lin, -np.inf)
        per_head = per_head.reshape(N_KV_HEADS, PAIRS_PER_KV, 2, BLOCK, 2 * BLOCK)
        stacked = per_head.transpose(0, 1, 3, 2, 4).reshape(N_KV_HEADS, PAIRS_PER_KV * BLOCK, 4 * BLOCK)
        tables.append(stacked)
    return jnp.asarray(np.stack(tables).astype(np.float32))


def _mixer(h, sinks, gm, win, wau, wmix, pscale, wpu, wout, bias, passengers, *, tm):
    n = h.shape[0]
    steps = n // tm
    kvw = N_KV_HEADS * LANES
    cast_specs = [_passenger_spec(w.shape, steps) for w in passengers]
    out = pl.pallas_call(
        functools.partial(_mixer_kernel, tm=tm, n_cast=len(passengers)),
        grid=(steps,),
        in_specs=[
            pl.BlockSpec(memory_space=pltpu.SMEM),
            pl.BlockSpec((tm, D_MODEL), lambda i: (i, 0)),
            _resident((1, D_MODEL)),
            _resident((D_MODEL, IN_WIDTH)),
            _resident((ATTN_WIDTH, D_MODEL)),
            _resident((POOL_WIDTH, POOL_WIDTH)),
            _resident((1, POOL_WIDTH)),
            _resident((POOL_WIDTH, D_MODEL)),
            _resident((D_MODEL, D_MODEL)),
            _resident((2, N_KV_HEADS, PAIRS_PER_KV * BLOCK, 4 * BLOCK)),
        ] + cast_specs,
        out_specs=[pl.BlockSpec((tm, D_MODEL), lambda i: (i, 0))] + cast_specs,
        out_shape=[jax.ShapeDtypeStruct((n, D_MODEL), F32)]
        + [jax.ShapeDtypeStruct(w.shape, BF16) for w in passengers],
        scratch_shapes=[
            pltpu.VMEM((tm, D_MODEL), BF16),
            pltpu.VMEM((tm, ATTN_WIDTH), BF16),
            pltpu.VMEM((LANES, tm + BLOCK), BF16),
            pltpu.VMEM((tm + BLOCK, 2 * kvw), BF16),
            pltpu.VMEM((tm + BLOCK, 2 * kvw), BF16),
            pltpu.VMEM((tm + POOL_HIST, POOL_WIDTH), F32),
            pltpu.VMEM((tm, ATTN_WIDTH), BF16),
            pltpu.VMEM((tm, POOL_WIDTH), BF16),
            pltpu.VMEM((tm, D_MODEL), BF16),
            pltpu.VMEM((tm, 2 * D_MODEL), F32),
        ],
        compiler_params=pltpu.CompilerParams(
            dimension_semantics=("arbitrary",), vmem_limit_bytes=VMEM_LIMIT_BYTES),
        name="mixer",
    )(sinks, h, gm, win, wau, wmix, pscale, wpu, wout, bias, *passengers)
    return out[0], out[1:]


def _block_diagonal(blocks):
    g, n, _ = blocks.shape
    out = jnp.zeros((g * n, g * n), blocks.dtype)
    for gi in range(g):
        out = out.at[gi * n:(gi + 1) * n, gi * n:(gi + 1) * n].set(blocks[gi])
    return out


def kernel(x, ffn1_norm, ffn1_w_up, ffn1_w_down, mix_norm, w_in, sinks, w_attn_up, pool_w_mix, pool_scale,
           w_pool_up, w_out, ffn2_norm, ffn2_w_up, ffn2_w_down, final_norm):
    batch, seq, d = x.shape
    depth = ffn1_norm.shape[0]
    assert (seq, d) == (SEQ, D_MODEL)
    h = x.reshape(batch * seq, d)
    bias = _attention_bias()
    ffn1_w = (ffn1_w_up[0].astype(BF16), ffn1_w_down[0].astype(BF16))
    for l in range(depth):
        last = l == depth - 1
        h, (win, wau, wmix, wpu, wout) = _ffn(
            h, ffn1_norm[l].reshape(1, d), *ffn1_w, None,
            [w_in[l], w_attn_up[l], _block_diagonal(pool_w_mix[l]), w_pool_up[l], w_out[l]], tm=FFN_ROWS)
        h, ffn2_w = _mixer(h, sinks[l], mix_norm[l].reshape(1, d), win, wau, wmix,
                           pool_scale[l].reshape(1, POOL_WIDTH), wpu, wout, bias,
                           [ffn2_w_up[l], ffn2_w_down[l]], tm=MIXER_ROWS)
        h, ffn1_w = _ffn(h, ffn2_norm[l].reshape(1, d), *ffn2_w, final_norm.reshape(1, d) if last else None,
                         [] if last else [ffn1_w_up[l + 1], ffn1_w_down[l + 1]], tm=FFN_ROWS)
    return h.reshape(batch, seq, d)
```

```python
import functools
import math

import jax
import jax.numpy as jnp
import numpy as np
from jax import lax
from jax.experimental import pallas as pl
from jax.experimental.pallas import tpu as pltpu

D_MODEL = 1024
SEQ = 8192
N_Q_HEADS = 16
N_KV_HEADS = 2
HEAD_DIM = 64
Q_PER_KV = N_Q_HEADS // N_KV_HEADS
WINDOW = 128
BLOCK = 128
ATTN_WIDTH = N_Q_HEADS * HEAD_DIM
KV_WIDTH = N_KV_HEADS * HEAD_DIM
POOL_WINDOWS = (2, 4, 8, 16)
POOL_WIDTH = 512
POOL_GROUP = POOL_WIDTH // len(POOL_WINDOWS)
D_FF = 2816
NORM_EPS = 1e-6

LANES = 128
MXU_DIM = 256
BF16_TILE_ROWS = 16
VMEM_LIMIT_BYTES = 60 * 1024 * 1024

FFN_ROWS = 1024
MIXER_ROWS = 1024
ROW_GROUP = MXU_DIM
FF_CHUNK = MXU_DIM
N_FF_CHUNKS = D_FF // FF_CHUNK
POOL_HIST = 16
GATE_ROWS = MXU_DIM
GATE_COLS = 2 * MXU_DIM
LOG2E = math.log2(math.e)

_K0 = ATTN_WIDTH
_V0 = _K0 + KV_WIDTH
_Z0 = _V0 + KV_WIDTH
_GA0 = _Z0 + POOL_WIDTH
_GP0 = _GA0 + D_MODEL
IN_WIDTH = _GP0 + D_MODEL

F32 = jnp.float32
BF16 = jnp.bfloat16


def _resident(shape):
    nd = len(shape)
    return pl.BlockSpec(shape, lambda i: (0,) * nd, pipeline_mode=pl.Buffered(1))


def _rmsnorm(x, g):
    return x * lax.rsqrt(jnp.mean(x * x, axis=-1, keepdims=True) + NORM_EPS) * g


def _dot(a, b):
    return jnp.dot(a, b, preferred_element_type=F32)


def _dot_nt(a, b):
    return lax.dot_general(a, b, (((1,), (1,)), ((), ())), preferred_element_type=F32)


def _passenger_spec(shape, steps):
    rows, cols = shape
    assert rows % BF16_TILE_ROWS == 0
    nblk = steps
    while rows % nblk or (rows // nblk) % BF16_TILE_ROWS:
        nblk //= 2
    per = steps // nblk
    return pl.BlockSpec((rows // nblk, cols), lambda i: (i // per, 0))


def _cast_passengers(src_refs, dst_refs):
    for src, dst in zip(src_refs, dst_refs):
        dst[...] = src[...].astype(BF16)


def _ffn_kernel(x_ref, g_ref, wup_ref, wdn_ref, *rest, final_norm, n_cast):
    rest = list(rest)
    gf_ref = rest.pop(0) if final_norm else None
    cast_in, (o_ref, *cast_out), (xn_ref, act_ref) = rest[:n_cast], rest[n_cast:2 * n_cast + 1], rest[-2:]
    tm = x_ref.shape[0]

    def swiglu_chunk(c, rows):
        cs = slice(c * FF_CHUNK, (c + 1) * FF_CHUNK)
        a = _dot(xn_ref[rows, :], wup_ref[:, cs])
        b = _dot(xn_ref[rows, :], wup_ref[:, D_FF + c * FF_CHUNK:D_FF + (c + 1) * FF_CHUNK])
        act_ref[rows, cs] = (a * jax.nn.sigmoid(a) * b).astype(BF16)

    for r in range(tm // ROW_GROUP):
        rows = slice(r * ROW_GROUP, (r + 1) * ROW_GROUP)
        xn_ref[rows, :] = _rmsnorm(x_ref[rows, :], g_ref[...]).astype(BF16)
        swiglu_chunk(0, rows)
    for c in range(1, N_FF_CHUNKS):
        swiglu_chunk(c, slice(None))
    _cast_passengers(cast_in, cast_out)
    for r in range(tm // ROW_GROUP):
        rows = slice(r * ROW_GROUP, (r + 1) * ROW_GROUP)
        y = x_ref[rows, :] + 0.5 * _dot(act_ref[rows, :], wdn_ref[...])
        if final_norm:
            y = _rmsnorm(y, gf_ref[...])
        o_ref[rows, :] = y


def _ffn(x, g, wup, wdn, gf, passengers, *, tm):
    n = x.shape[0]
    steps = n // tm
    final_norm = gf is not None
    in_specs = [
        pl.BlockSpec((tm, D_MODEL), lambda i: (i, 0)),
        _resident((1, D_MODEL)),
        _resident((D_MODEL, 2 * D_FF)),
        _resident((D_FF, D_MODEL)),
    ]
    args = [x, g, wup, wdn]
    if final_norm:
        in_specs.append(_resident((1, D_MODEL)))
        args.append(gf)
    cast_specs = [_passenger_spec(w.shape, steps) for w in passengers]
    out = pl.pallas_call(
        functools.partial(_ffn_kernel, final_norm=final_norm, n_cast=len(passengers)),
        grid=(steps,),
        in_specs=in_specs + cast_specs,
        out_specs=[pl.BlockSpec((tm, D_MODEL), lambda i: (i, 0))] + cast_specs,
        out_shape=[jax.ShapeDtypeStruct((n, D_MODEL), F32)]
        + [jax.ShapeDtypeStruct(w.shape, BF16) for w in passengers],
        scratch_shapes=[pltpu.VMEM((tm, D_MODEL), BF16), pltpu.VMEM((tm, D_FF), BF16)],
        compiler_params=pltpu.CompilerParams(
            dimension_semantics=("arbitrary",), vmem_limit_bytes=VMEM_LIMIT_BYTES),
        name="ffn_final" if final_norm else "ffn",
    )(*args, *passengers)
    return out[0], out[1:]


def _mixer_kernel(sinks_ref, h_ref, gm_ref, win_ref, wqT_ref, wvT_ref, wau_ref, wmix_ref, pscale_ref,
                  wpu_ref, wout_ref, bias_ref, *rest, tm, n_cast):
    cast_in, (o_ref, *cast_out) = rest[:n_cast], rest[n_cast:2 * n_cast + 1]
    (u_scr, qT_scr, k_scr, vT_scr, z_scr, attn_scr, pool_scr, mrg_scr, gate_scr) = rest[2 * n_cast + 1:]
    i = pl.program_id(0)
    tiles_per_seq = SEQ // tm
    blocks_per_tile = tm // BLOCK
    gate_row_groups = tm // GATE_ROWS
    assert gate_row_groups * (2 * D_MODEL // GATE_COLS) == blocks_per_tile * N_KV_HEADS
    tile_in_seq = i % tiles_per_seq
    seq_start = tile_in_seq == 0

    @pl.when(seq_start)
    def _():
        k_scr[0:BLOCK, :] = jnp.zeros((BLOCK, KV_WIDTH), BF16)
        vT_scr[:, 0:BLOCK] = jnp.zeros((KV_WIDTH, BLOCK), BF16)
        z_scr[0:POOL_HIST, :] = jnp.zeros((POOL_HIST, POOL_WIDTH), F32)

    @pl.when(jnp.logical_not(seq_start))
    def _():
        k_scr[0:BLOCK, :] = k_scr[tm:tm + BLOCK, :]
        vT_scr[:, 0:BLOCK] = vT_scr[:, tm:tm + BLOCK]
        z_scr[0:POOL_HIST, :] = z_scr[tm:tm + POOL_HIST, :]

    for r in range(tm // ROW_GROUP):
        rows = slice(r * ROW_GROUP, (r + 1) * ROW_GROUP)
        u_r = _rmsnorm(h_ref[rows, :], gm_ref[...]).astype(BF16)
        u_scr[rows, :] = u_r
        qT_scr[:, rows] = (_dot_nt(wqT_ref[...], u_r) * (HEAD_DIM ** -0.5 * LOG2E)).astype(BF16)
    u = u_scr[...]
    k_scr[BLOCK:, :] = _dot(u, win_ref[:, _K0:_V0]).astype(BF16)
    vT_scr[:, BLOCK:] = _dot_nt(wvT_ref[...], u).astype(BF16)
    z_scr[POOL_HIST:, :] = _dot(u, win_ref[:, _Z0:_GA0])

    qzero = jnp.zeros((HEAD_DIM, BLOCK), BF16)
    ones_rows = jnp.ones((BF16_TILE_ROWS, 2 * BLOCK), BF16)

    for b in range(blocks_per_tile):
        r0 = b * BLOCK
        first_block = seq_start.astype(jnp.int32) if b == 0 else 0
        k_band = k_scr[r0:r0 + 2 * BLOCK, :]
        for hk in range(N_KV_HEADS):
            slabs = []
            for g in range(Q_PER_KV):
                d0 = (hk * Q_PER_KV + g) * HEAD_DIM
                qT = qT_scr[d0:d0 + HEAD_DIM, r0:r0 + BLOCK]
                slabs.append(jnp.concatenate([qT, qzero] if hk == 0 else [qzero, qT], axis=0))
            sT = _dot(k_band, jnp.concatenate(slabs, axis=1)) + bias_ref[first_block, hk]
            step = b * N_KV_HEADS + hk
            grow = slice((step % gate_row_groups) * GATE_ROWS, (step % gate_row_groups + 1) * GATE_ROWS)
            gcol = (step // gate_row_groups) * GATE_COLS
            gate_scr[grow, gcol:gcol + GATE_COLS] = _dot(
                u_scr[grow, :], win_ref[:, _GA0 + gcol:_GA0 + gcol + GATE_COLS])
            eT, sink_terms = [], []
            for g in range(Q_PER_KV):
                sg = sT[:, g * BLOCK:(g + 1) * BLOCK]
                sink = sinks_ref[hk * Q_PER_KV + g] * LOG2E
                m = jnp.maximum(jnp.max(sg, axis=0, keepdims=True), sink)
                eT.append(jnp.exp2(sg - m).astype(BF16))
                sink_terms.append(jnp.exp2(sink - m))
            vT_ext = jnp.concatenate(
                [vT_scr[hk * HEAD_DIM:(hk + 1) * HEAD_DIM, r0:r0 + 2 * BLOCK], ones_rows], axis=0)
            outT = _dot(vT_ext, jnp.concatenate(eT, axis=1))
            den = outT[HEAD_DIM:HEAD_DIM + 1, :] + jnp.concatenate(sink_terms, axis=1)
            oT = outT[:HEAD_DIM, :] / den
            stacked = jnp.concatenate([oT[:, g * BLOCK:(g + 1) * BLOCK] for g in range(Q_PER_KV)], axis=0)
            q0 = hk * Q_PER_KV * HEAD_DIM
            attn_scr[r0:r0 + BLOCK, q0:q0 + Q_PER_KV * HEAD_DIM] = stacked.T.astype(BF16)

    t = tile_in_seq * tm + lax.broadcasted_iota(jnp.int32, (tm, POOL_GROUP), 0)
    for gi, w in enumerate(POOL_WINDOWS):
        cs = slice(gi * POOL_GROUP, (gi + 1) * POOL_GROUP)
        acc = z_scr[:, cs]
        span = 1
        while span < w:
            acc = acc + pltpu.roll(acc, span, 0)
            span *= 2
        cnt = jnp.minimum(t + 1, w).astype(F32)
        pool_scr[:, cs] = (acc[POOL_HIST:] / cnt - z_scr[POOL_HIST:, cs]).astype(BF16)
    mixed = (_dot(pool_scr[...], wmix_ref[...]) * pscale_ref[...]).astype(BF16)

    half = D_MODEL // 2
    for c in range(2):
        cs = slice(c * half, (c + 1) * half)
        a = _dot(attn_scr[...], wau_ref[:, cs])
        p = _dot(mixed, wpu_ref[:, cs])
        ga = jax.nn.sigmoid(gate_scr[:, c * half:(c + 1) * half])
        gp = jax.nn.sigmoid(gate_scr[:, D_MODEL + c * half:D_MODEL + (c + 1) * half])
        mrg_scr[:, cs] = (ga * a + gp * p).astype(BF16)
    _cast_passengers(cast_in, cast_out)
    for r in range(tm // ROW_GROUP):
        rows = slice(r * ROW_GROUP, (r + 1) * ROW_GROUP)
        o_ref[rows, :] = h_ref[rows, :] + _dot(mrg_scr[rows, :], wout_ref[...])


def _attention_bias():
    hidx = np.arange(1, N_Q_HEADS + 1, dtype=np.float32)
    slopes = (2.0 ** (-8.0 * hidx / N_Q_HEADS)).astype(np.float32)
    qi = np.arange(BLOCK)[:, None] + BLOCK
    kj = np.arange(2 * BLOCK)[None, :]
    dist = qi - kj
    valid = (dist >= 0) & (dist < WINDOW)
    valid_first = valid & (kj >= BLOCK)
    lin = -(slopes[:, None, None] * dist.astype(np.float32)[None]).astype(np.float64) * LOG2E
    tables = []
    for vis in (valid, valid_first):
        per_head = np.where(vis[None], lin, -np.inf)
        per_head = per_head.reshape(N_KV_HEADS, Q_PER_KV, BLOCK, 2 * BLOCK)
        tables.append(per_head.transpose(0, 3, 1, 2).reshape(N_KV_HEADS, 2 * BLOCK, Q_PER_KV * BLOCK))
    return jnp.asarray(np.stack(tables).astype(np.float32))


def _mixer(h, sinks, gm, win, wqT, wvT, wau, wmix, pscale, wpu, wout, bias, passengers, *, tm):
    n = h.shape[0]
    steps = n // tm
    cast_specs = [_passenger_spec(w.shape, steps) for w in passengers]
    out = pl.pallas_call(
        functools.partial(_mixer_kernel, tm=tm, n_cast=len(passengers)),
        grid=(steps,),
        in_specs=[
            pl.BlockSpec(memory_space=pltpu.SMEM),
            pl.BlockSpec((tm, D_MODEL), lambda i: (i, 0)),
            _resident((1, D_MODEL)),
            _resident((D_MODEL, IN_WIDTH)),
            _resident((ATTN_WIDTH, D_MODEL)),
            _resident((KV_WIDTH, D_MODEL)),
            _resident((ATTN_WIDTH, D_MODEL)),
            _resident((POOL_WIDTH, POOL_WIDTH)),
            _resident((1, POOL_WIDTH)),
            _resident((POOL_WIDTH, D_MODEL)),
            _resident((D_MODEL, D_MODEL)),
            _resident((2, N_KV_HEADS, 2 * BLOCK, Q_PER_KV * BLOCK)),
        ] + cast_specs,
        out_specs=[pl.BlockSpec((tm, D_MODEL), lambda i: (i, 0))] + cast_specs,
        out_shape=[jax.ShapeDtypeStruct((n, D_MODEL), F32)]
        + [jax.ShapeDtypeStruct(w.shape, BF16) for w in passengers],
        scratch_shapes=[
            pltpu.VMEM((tm, D_MODEL), BF16),
            pltpu.VMEM((ATTN_WIDTH, tm), BF16),
            pltpu.VMEM((tm + BLOCK, KV_WIDTH), BF16),
            pltpu.VMEM((KV_WIDTH, tm + BLOCK), BF16),
            pltpu.VMEM((tm + POOL_HIST, POOL_WIDTH), F32),
            pltpu.VMEM((tm, ATTN_WIDTH), BF16),
            pltpu.VMEM((tm, POOL_WIDTH), BF16),
            pltpu.VMEM((tm, D_MODEL), BF16),
            pltpu.VMEM((tm, 2 * D_MODEL), F32),
        ],
        compiler_params=pltpu.CompilerParams(
            dimension_semantics=("arbitrary",), vmem_limit_bytes=VMEM_LIMIT_BYTES),
        name="mixer",
    )(sinks, h, gm, win, wqT, wvT, wau, wmix, pscale, wpu, wout, bias, *passengers)
    return out[0], out[1:]


def _block_diagonal(blocks):
    g, n, _ = blocks.shape
    out = jnp.zeros((g * n, g * n), blocks.dtype)
    for gi in range(g):
        out = out.at[gi * n:(gi + 1) * n, gi * n:(gi + 1) * n].set(blocks[gi])
    return out


def kernel(x, ffn1_norm, ffn1_w_up, ffn1_w_down, mix_norm, w_in, sinks, w_attn_up, pool_w_mix, pool_scale,
           w_pool_up, w_out, ffn2_norm, ffn2_w_up, ffn2_w_down, final_norm):
    batch, seq, d = x.shape
    depth = ffn1_norm.shape[0]
    assert (seq, d) == (SEQ, D_MODEL)
    h = x.reshape(batch * seq, d)
    bias = _attention_bias()
    ffn1_w = (ffn1_w_up[0].astype(BF16), ffn1_w_down[0].astype(BF16))
    for l in range(depth):
        last = l == depth - 1
        h, (win, wau, wmix, wpu, wout) = _ffn(
            h, ffn1_norm[l].reshape(1, d), *ffn1_w, None,
            [w_in[l], w_attn_up[l], _block_diagonal(pool_w_mix[l]), w_pool_up[l], w_out[l]], tm=FFN_ROWS)
        wqT = w_in[l][:, :_K0].T.astype(BF16)
        wvT = w_in[l][:, _V0:_Z0].T.astype(BF16)
        h, ffn2_w = _mixer(h, sinks[l], mix_norm[l].reshape(1, d), win, wqT, wvT, wau, wmix,
                           pool_scale[l].reshape(1, POOL_WIDTH), wpu, wout, bias,
                           [ffn2_w_up[l], ffn2_w_down[l]], tm=MIXER_ROWS)
        h, ffn1_w = _ffn(h, ffn2_norm[l].reshape(1, d), *ffn2_w, final_norm.reshape(1, d) if last else None,
                         [] if last else [ffn1_w_up[l + 1], ffn1_w_down[l + 1]], tm=FFN_ROWS)
    return h.reshape(batch, seq, d)
```

```python
import functools
import math

import jax
import jax.numpy as jnp
import numpy as np
from jax import lax
from jax.experimental import pallas as pl
from jax.experimental.pallas import tpu as pltpu

D_MODEL = 1024
SEQ = 8192
N_Q_HEADS = 16
N_KV_HEADS = 2
HEAD_DIM = 64
Q_PER_KV = N_Q_HEADS // N_KV_HEADS
WINDOW = 128
BLOCK = 128
ATTN_WIDTH = N_Q_HEADS * HEAD_DIM
KV_WIDTH = N_KV_HEADS * HEAD_DIM
POOL_WINDOWS = (2, 4, 8, 16)
POOL_WIDTH = 512
POOL_GROUP = POOL_WIDTH // len(POOL_WINDOWS)
D_FF = 2816
NORM_EPS = 1e-6

LANES = 128
MXU_DIM = 256
BF16_TILE_ROWS = 16
VMEM_LIMIT_BYTES = 60 * 1024 * 1024

FFN_ROWS = 1024
MIXER_ROWS = 1024
ROW_GROUP = MXU_DIM
FF_CHUNK = MXU_DIM
FFN_CAST_STEPS = 8
N_FF_CHUNKS = D_FF // FF_CHUNK
POOL_HIST = 16
PAIRS_PER_KV = Q_PER_KV // 2
GATE_ROWS = MXU_DIM
GATE_COLS = 2 * MXU_DIM
LOG2E = math.log2(math.e)

_K0 = ATTN_WIDTH
_V0 = _K0 + KV_WIDTH
_Z0 = _V0 + KV_WIDTH
_GA0 = _Z0 + POOL_WIDTH
_GP0 = _GA0 + D_MODEL
IN_WIDTH = _GP0 + D_MODEL

F32 = jnp.float32
BF16 = jnp.bfloat16


def _resident(shape):
    nd = len(shape)
    return pl.BlockSpec(shape, lambda i: (0,) * nd, pipeline_mode=pl.Buffered(1))


def _rmsnorm(x, g):
    return x * lax.rsqrt(jnp.mean(x * x, axis=-1, keepdims=True) + NORM_EPS) * g


def _dot(a, b):
    return jnp.dot(a, b, preferred_element_type=F32)


def _passenger_spec(shape, steps, first_step=0):
    rows, cols = shape
    assert rows % BF16_TILE_ROWS == 0
    nblk = steps
    while rows % nblk or (rows // nblk) % BF16_TILE_ROWS:
        nblk //= 2
    per = steps // nblk
    return pl.BlockSpec((rows // nblk, cols), lambda i: (jnp.maximum(i - first_step, 0) // per, 0))


def _cast_passengers(src_refs, dst_refs):
    for src, dst in zip(src_refs, dst_refs):
        dst[...] = src[...].astype(BF16)


def _ffn_kernel(x_ref, g_ref, wup_in, wdn_in, *rest, final_norm, n_cast, pre_steps):
    rest = list(rest)
    gf_ref = rest.pop(0) if final_norm else None
    cast_in, (o_ref, *cast_out) = rest[:n_cast], rest[n_cast:2 * n_cast + 1]
    scratch = rest[2 * n_cast + 1:]
    if pre_steps:
        xn_ref, act_ref, wup_ref, wdn_ref = scratch
        i = pl.program_id(0)

        @pl.when(i < pre_steps)
        def _():
            for src, dst in ((wup_in, wup_ref), (wdn_in, wdn_ref)):
                rows = src.shape[0]
                dst[pl.ds(pl.multiple_of(i * rows, BF16_TILE_ROWS), rows), :] = src[...].astype(BF16)

        @pl.when(i >= pre_steps)
        def _():
            _ffn_tile(x_ref, g_ref, wup_ref, wdn_ref, gf_ref, cast_in, o_ref, cast_out, xn_ref, act_ref)
    else:
        xn_ref, act_ref = scratch
        _ffn_tile(x_ref, g_ref, wup_in, wdn_in, gf_ref, cast_in, o_ref, cast_out, xn_ref, act_ref)


def _ffn_tile(x_ref, g_ref, wup_ref, wdn_ref, gf_ref, cast_in, o_ref, cast_out, xn_ref, act_ref):
    final_norm = gf_ref is not None
    tm = x_ref.shape[0]

    def swiglu_chunk(c, rows):
        cs = slice(c * FF_CHUNK, (c + 1) * FF_CHUNK)
        a = _dot(xn_ref[rows, :], wup_ref[:, cs])
        b = _dot(xn_ref[rows, :], wup_ref[:, D_FF + c * FF_CHUNK:D_FF + (c + 1) * FF_CHUNK])
        act_ref[rows, cs] = (a * jax.nn.sigmoid(a) * b).astype(BF16)

    for r in range(tm // ROW_GROUP):
        rows = slice(r * ROW_GROUP, (r + 1) * ROW_GROUP)
        xn_ref[rows, :] = _rmsnorm(x_ref[rows, :], g_ref[...]).astype(BF16)
        swiglu_chunk(0, rows)
    for c in range(1, N_FF_CHUNKS):
        swiglu_chunk(c, slice(None))
    _cast_passengers(cast_in, cast_out)
    for r in range(tm // ROW_GROUP):
        rows = slice(r * ROW_GROUP, (r + 1) * ROW_GROUP)
        y = x_ref[rows, :] + 0.5 * _dot(act_ref[rows, :], wdn_ref[...])
        if final_norm:
            y = _rmsnorm(y, gf_ref[...])
        o_ref[rows, :] = y


def _ffn(x, g, wup, wdn, gf, passengers, *, tm):
    n = x.shape[0]
    steps = n // tm
    final_norm = gf is not None
    pre = FFN_CAST_STEPS if wup.dtype == F32 else 0
    row_tile = lambda i: (jnp.maximum(i - pre, 0), 0)
    in_specs = [pl.BlockSpec((tm, D_MODEL), row_tile), _resident((1, D_MODEL))]
    scratch = [pltpu.VMEM((tm, D_MODEL), BF16), pltpu.VMEM((tm, D_FF), BF16)]
    if pre:
        cast_block = lambda i: (jnp.minimum(i, pre - 1), 0)
        in_specs += [pl.BlockSpec((D_MODEL // pre, 2 * D_FF), cast_block),
                     pl.BlockSpec((D_FF // pre, D_MODEL), cast_block)]
        scratch += [pltpu.VMEM((D_MODEL, 2 * D_FF), BF16), pltpu.VMEM((D_FF, D_MODEL), BF16)]
    else:
        in_specs += [_resident((D_MODEL, 2 * D_FF)), _resident((D_FF, D_MODEL))]
    args = [x, g, wup, wdn]
    if final_norm:
        in_specs.append(_resident((1, D_MODEL)))
        args.append(gf)
    cast_specs = [_passenger_spec(w.shape, steps, pre) for w in passengers]
    out = pl.pallas_call(
        functools.partial(_ffn_kernel, final_norm=final_norm, n_cast=len(passengers), pre_steps=pre),
        grid=(pre + steps,),
        in_specs=in_specs + cast_specs,
        out_specs=[pl.BlockSpec((tm, D_MODEL), row_tile)] + cast_specs,
        out_shape=[jax.ShapeDtypeStruct((n, D_MODEL), F32)]
        + [jax.ShapeDtypeStruct(w.shape, BF16) for w in passengers],
        scratch_shapes=scratch,
        compiler_params=pltpu.CompilerParams(
            dimension_semantics=("arbitrary",), vmem_limit_bytes=VMEM_LIMIT_BYTES),
        name="ffn_final" if final_norm else "ffn",
    )(*args, *passengers)
    return out[0], out[1:]


def _mixer_kernel(sinks_ref, h_ref, gm_ref, win_ref, wau_ref, wmix_ref, pscale_ref,
                  wpu_ref, wout_ref, bias_ref, *rest, tm, n_cast):
    cast_in, (o_ref, *cast_out) = rest[:n_cast], rest[n_cast:2 * n_cast + 1]
    (u_scr, q_scr, kT_scr, vt_scr, vb_scr, z_scr, attn_scr, pool_scr, mrg_scr, gate_scr) = rest[2 * n_cast + 1:]
    i = pl.program_id(0)
    tiles_per_seq = SEQ // tm
    blocks_per_tile = tm // BLOCK
    gate_row_groups = tm // GATE_ROWS
    assert gate_row_groups * (2 * D_MODEL // GATE_COLS) == blocks_per_tile * N_KV_HEADS
    tile_in_seq = i % tiles_per_seq
    seq_start = tile_in_seq == 0

    @pl.when(seq_start)
    def _():
        kT_scr[:, 0:BLOCK] = jnp.zeros((LANES, BLOCK), BF16)
        for scr in (vt_scr, vb_scr):
            scr[0:BLOCK, :] = jnp.zeros((BLOCK, scr.shape[1]), scr.dtype)
        z_scr[0:POOL_HIST, :] = jnp.zeros((POOL_HIST, POOL_WIDTH), F32)

    @pl.when(jnp.logical_not(seq_start))
    def _():
        kT_scr[:, 0:BLOCK] = kT_scr[:, tm:tm + BLOCK]
        for scr in (vt_scr, vb_scr):
            scr[0:BLOCK, :] = scr[tm:tm + BLOCK, :]
        z_scr[0:POOL_HIST, :] = z_scr[tm:tm + POOL_HIST, :]

    for r in range(tm // ROW_GROUP):
        rows = slice(r * ROW_GROUP, (r + 1) * ROW_GROUP)
        u_r = _rmsnorm(h_ref[rows, :], gm_ref[...]).astype(BF16)
        u_scr[rows, :] = u_r
        q_scr[rows, :] = (_dot(u_r, win_ref[:, 0:_K0]) * (HEAD_DIM ** -0.5 * LOG2E)).astype(BF16)
    u = u_scr[...]

    kv = _dot(u, win_ref[:, _K0:_Z0])
    kT_scr[:, BLOCK:] = kv[:, :LANES].T.astype(BF16)

    low = lax.broadcasted_iota(jnp.int32, (tm, LANES), 1) < HEAD_DIM
    v01 = kv[:, LANES:]
    v10 = pltpu.roll(v01, HEAD_DIM, 1)
    zero = jnp.zeros_like(v01)
    v_tops = (jnp.where(low, v01, zero), jnp.where(low, v10, zero))
    v_bots = (jnp.where(low, zero, v10), jnp.where(low, zero, v01))
    lane = lax.broadcasted_iota(jnp.int32, (tm + BLOCK, LANES), 1)
    ones_top = jnp.where(lane < HEAD_DIM, 1.0, 0.0).astype(BF16)
    ones_bot = jnp.where(lane >= HEAD_DIM, 1.0, 0.0).astype(BF16)
    for hk in range(N_KV_HEADS):
        c0 = hk * 2 * LANES
        vt_scr[BLOCK:, c0:c0 + LANES] = v_tops[hk].astype(BF16)
        vb_scr[BLOCK:, c0:c0 + LANES] = v_bots[hk].astype(BF16)
        vt_scr[:, c0 + LANES:c0 + 2 * LANES] = ones_top
        vb_scr[:, c0 + LANES:c0 + 2 * LANES] = ones_bot
    z_scr[POOL_HIST:, :] = _dot(u, win_ref[:, _Z0:_GA0])

    half_lane = lax.broadcasted_iota(jnp.int32, (BLOCK, LANES), 1) < HEAD_DIM
    kzero = jnp.zeros((HEAD_DIM, 2 * BLOCK), BF16)

    def finish_step(e_even, e_odd, sink_terms, vt, vb, r0, q0):
        pv = (_dot(jnp.concatenate(e_even, axis=0), vt)
              + _dot(jnp.concatenate(e_odd, axis=0), vb))
        for p in range(PAIRS_PER_KV):
            rows = slice(p * BLOCK, (p + 1) * BLOCK)
            out = pv[rows, :LANES] / (pv[rows, LANES:] + sink_terms[p])
            attn_scr[r0:r0 + BLOCK, q0 + p * LANES:q0 + (p + 1) * LANES] = out.astype(BF16)

    pending = None
    for b in range(blocks_per_tile):
        r0 = b * BLOCK
        first_block = seq_start.astype(jnp.int32) if b == 0 else 0
        for hk in range(N_KV_HEADS):
            kTb = kT_scr[hk * HEAD_DIM:(hk + 1) * HEAD_DIM, r0:r0 + 2 * BLOCK]
            k_even_odd = jnp.concatenate(
                [jnp.concatenate([kTb, kzero], axis=1), jnp.concatenate([kzero, kTb], axis=1)], axis=0)
            vt = vt_scr[r0:r0 + 2 * BLOCK, hk * 2 * LANES:(hk + 1) * 2 * LANES]
            vb = vb_scr[r0:r0 + 2 * BLOCK, hk * 2 * LANES:(hk + 1) * 2 * LANES]
            q0 = hk * Q_PER_KV * HEAD_DIM
            qs = jnp.concatenate(
                [q_scr[r0:r0 + BLOCK, q0 + p * LANES:q0 + (p + 1) * LANES] for p in range(PAIRS_PER_KV)],
                axis=0)
            s = _dot(qs, k_even_odd) + bias_ref[first_block, hk]
            s_even = s[:, :2 * BLOCK]
            s_odd = s[:, 2 * BLOCK:]
            step = b * N_KV_HEADS + hk
            grow = slice((step % gate_row_groups) * GATE_ROWS, (step % gate_row_groups + 1) * GATE_ROWS)
            gcol = (step // gate_row_groups) * GATE_COLS
            gate_scr[grow, gcol:gcol + GATE_COLS] = _dot(
                u_scr[grow, :], win_ref[:, _GA0 + gcol:_GA0 + gcol + GATE_COLS])
            e_even, e_odd, sink_terms = [], [], []
            for p in range(PAIRS_PER_KV):
                rows = slice(p * BLOCK, (p + 1) * BLOCK)
                head0 = hk * Q_PER_KV + 2 * p
                sink0 = sinks_ref[head0] * LOG2E
                sink1 = sinks_ref[head0 + 1] * LOG2E
                s0 = s_even[rows]
                s1 = s_odd[rows]
                m0 = jnp.maximum(jnp.max(s0, axis=-1, keepdims=True), sink0)
                m1 = jnp.maximum(jnp.max(s1, axis=-1, keepdims=True), sink1)
                e_even.append(jnp.exp2(s0 - m0).astype(BF16))
                e_odd.append(jnp.exp2(s1 - m1).astype(BF16))
                sink_terms.append(jnp.exp2(jnp.where(half_lane, sink0 - m0, sink1 - m1)))
            if pending is not None:
                finish_step(*pending)
            pending = (e_even, e_odd, sink_terms, vt, vb, r0, q0)
    finish_step(*pending)

    t = tile_in_seq * tm + lax.broadcasted_iota(jnp.int32, (tm, POOL_GROUP), 0)
    for gi, w in enumerate(POOL_WINDOWS):
        cs = slice(gi * POOL_GROUP, (gi + 1) * POOL_GROUP)
        acc = z_scr[:, cs]
        span = 1
        while span < w:
            acc = acc + pltpu.roll(acc, span, 0)
            span *= 2
        cnt = jnp.minimum(t + 1, w).astype(F32)
        pool_scr[:, cs] = (acc[POOL_HIST:] / cnt - z_scr[POOL_HIST:, cs]).astype(BF16)
    mixed = (_dot(pool_scr[...], wmix_ref[...]) * pscale_ref[...]).astype(BF16)

    half = D_MODEL // 2
    for c in range(2):
        cs = slice(c * half, (c + 1) * half)
        a = _dot(attn_scr[...], wau_ref[:, cs])
        p = _dot(mixed, wpu_ref[:, cs])
        ga = jax.nn.sigmoid(gate_scr[:, c * half:(c + 1) * half])
        gp = jax.nn.sigmoid(gate_scr[:, D_MODEL + c * half:D_MODEL + (c + 1) * half])
        mrg_scr[:, cs] = (ga * a + gp * p).astype(BF16)
    _cast_passengers(cast_in, cast_out)
    for r in range(tm // ROW_GROUP):
        rows = slice(r * ROW_GROUP, (r + 1) * ROW_GROUP)
        o_ref[rows, :] = h_ref[rows, :] + _dot(mrg_scr[rows, :], wout_ref[...])


def _attention_bias():
    hidx = np.arange(1, N_Q_HEADS + 1, dtype=np.float32)
    slopes = (2.0 ** (-8.0 * hidx / N_Q_HEADS)).astype(np.float32)
    qi = np.arange(BLOCK)[:, None] + BLOCK
    kj = np.arange(2 * BLOCK)[None, :]
    dist = qi - kj
    valid = (dist >= 0) & (dist < WINDOW)
    valid_first = valid & (kj >= BLOCK)
    lin = -(slopes[:, None, None] * dist.astype(np.float32)[None]).astype(np.float64) * LOG2E
    tables = []
    for vis in (valid, valid_first):
        per_head = np.where(vis[None], lin, -np.inf)
        per_head = per_head.reshape(N_KV_HEADS, PAIRS_PER_KV, 2, BLOCK, 2 * BLOCK)
        stacked = per_head.transpose(0, 1, 3, 2, 4).reshape(N_KV_HEADS, PAIRS_PER_KV * BLOCK, 4 * BLOCK)
        tables.append(stacked)
    return jnp.asarray(np.stack(tables).astype(np.float32))


def _mixer(h, sinks, gm, win, wau, wmix, pscale, wpu, wout, bias, passengers, *, tm):
    n = h.shape[0]
    steps = n // tm
    kvw = N_KV_HEADS * LANES
    cast_specs = [_passenger_spec(w.shape, steps) for w in passengers]
    out = pl.pallas_call(
        functools.partial(_mixer_kernel, tm=tm, n_cast=len(passengers)),
        grid=(steps,),
        in_specs=[
            pl.BlockSpec(memory_space=pltpu.SMEM),
            pl.BlockSpec((tm, D_MODEL), lambda i: (i, 0)),
            _resident((1, D_MODEL)),
            _resident((D_MODEL, IN_WIDTH)),
            _resident((ATTN_WIDTH, D_MODEL)),
            _resident((POOL_WIDTH, POOL_WIDTH)),
            _resident((1, POOL_WIDTH)),
            _resident((POOL_WIDTH, D_MODEL)),
            _resident((D_MODEL, D_MODEL)),
            _resident((2, N_KV_HEADS, PAIRS_PER_KV * BLOCK, 4 * BLOCK)),
        ] + cast_specs,
        out_specs=[pl.BlockSpec((tm, D_MODEL), lambda i: (i, 0))] + cast_specs,
        out_shape=[jax.ShapeDtypeStruct((n, D_MODEL), F32)]
        + [jax.ShapeDtypeStruct(w.shape, BF16) for w in passengers],
        scratch_shapes=[
            pltpu.VMEM((tm, D_MODEL), BF16),
            pltpu.VMEM((tm, ATTN_WIDTH), BF16),
            pltpu.VMEM((LANES, tm + BLOCK), BF16),
            pltpu.VMEM((tm + BLOCK, 2 * kvw), BF16),
            pltpu.VMEM((tm + BLOCK, 2 * kvw), BF16),
            pltpu.VMEM((tm + POOL_HIST, POOL_WIDTH), F32),
            pltpu.VMEM((tm, ATTN_WIDTH), BF16),
            pltpu.VMEM((tm, POOL_WIDTH), BF16),
            pltpu.VMEM((tm, D_MODEL), BF16),
            pltpu.VMEM((tm, 2 * D_MODEL), F32),
        ],
        compiler_params=pltpu.CompilerParams(
            dimension_semantics=("arbitrary",), vmem_limit_bytes=VMEM_LIMIT_BYTES),
        name="mixer",
    )(sinks, h, gm, win, wau, wmix, pscale, wpu, wout, bias, *passengers)
    return out[0], out[1:]


def _block_diagonal(blocks):
    g, n, _ = blocks.shape
    out = jnp.zeros((g * n, g * n), blocks.dtype)
    for gi in range(g):
        out = out.at[gi * n:(gi + 1) * n, gi * n:(gi + 1) * n].set(blocks[gi])
    return out


def kernel(x, ffn1_norm, ffn1_w_up, ffn1_w_down, mix_norm, w_in, sinks, w_attn_up, pool_w_mix, pool_scale,
           w_pool_up, w_out, ffn2_norm, ffn2_w_up, ffn2_w_down, final_norm):
    batch, seq, d = x.shape
    depth = ffn1_norm.shape[0]
    assert (seq, d) == (SEQ, D_MODEL)
    h = x.reshape(batch * seq, d)
    bias = _attention_bias()
    ffn1_w = (ffn1_w_up[0], ffn1_w_down[0])
    for l in range(depth):
        last = l == depth - 1
        h, (win, wau, wmix, wpu, wout) = _ffn(
            h, ffn1_norm[l].reshape(1, d), *ffn1_w, None,
            [w_in[l], w_attn_up[l], _block_diagonal(pool_w_mix[l]), w_pool_up[l], w_out[l]], tm=FFN_ROWS)
        h, ffn2_w = _mixer(h, sinks[l], mix_norm[l].reshape(1, d), win, wau, wmix,
                           pool_scale[l].reshape(1, POOL_WIDTH), wpu, wout, bias,
                           [ffn2_w_up[l], ffn2_w_down[l]], tm=MIXER_ROWS)
        h, ffn1_w = _ffn(h, ffn2_norm[l].reshape(1, d), *ffn2_w, final_norm.reshape(1, d) if last else None,
                         [] if last else [ffn1_w_up[l + 1], ffn1_w_down[l + 1]], tm=FFN_ROWS)
    return h.reshape(batch, seq, d)
```

```python
import functools
import math

import jax
import jax.numpy as jnp
import numpy as np
from jax import lax
from jax.experimental import pallas as pl
from jax.experimental.pallas import tpu as pltpu

D_MODEL = 1024
SEQ = 8192
N_Q_HEADS = 16
N_KV_HEADS = 2
HEAD_DIM = 64
Q_PER_KV = N_Q_HEADS // N_KV_HEADS
WINDOW = 128
BLOCK = 128
ATTN_WIDTH = N_Q_HEADS * HEAD_DIM
KV_WIDTH = N_KV_HEADS * HEAD_DIM
POOL_WINDOWS = (2, 4, 8, 16)
POOL_WIDTH = 512
POOL_GROUP = POOL_WIDTH // len(POOL_WINDOWS)
D_FF = 2816
NORM_EPS = 1e-6

LANES = 128
MXU_DIM = 256
BF16_TILE_ROWS = 16
VMEM_LIMIT_BYTES = 60 * 1024 * 1024

FFN_ROWS = 1024
MIXER_ROWS = 1024
ROW_GROUP = MXU_DIM
FF_CHUNK = MXU_DIM
FFN_CAST_STEPS = 8
N_FF_CHUNKS = D_FF // FF_CHUNK
POOL_HIST = 16
PAIRS_PER_KV = Q_PER_KV // 2
GATE_ROWS = MXU_DIM
GATE_COLS = 2 * MXU_DIM
LOG2E = math.log2(math.e)

_K0 = ATTN_WIDTH
_V0 = _K0 + KV_WIDTH
_Z0 = _V0 + KV_WIDTH
_GA0 = _Z0 + POOL_WIDTH
_GP0 = _GA0 + D_MODEL
IN_WIDTH = _GP0 + D_MODEL

F32 = jnp.float32
BF16 = jnp.bfloat16


def _resident(shape):
    nd = len(shape)
    return pl.BlockSpec(shape, lambda i: (0,) * nd, pipeline_mode=pl.Buffered(1))


def _rmsnorm(x, g):
    return x * lax.rsqrt(jnp.mean(x * x, axis=-1, keepdims=True) + NORM_EPS) * g


def _dot(a, b):
    return jnp.dot(a, b, preferred_element_type=F32)


def _passenger_spec(shape, steps, first_step=0):
    rows, cols = shape
    assert rows % BF16_TILE_ROWS == 0
    nblk = steps
    while rows % nblk or (rows // nblk) % BF16_TILE_ROWS:
        nblk //= 2
    per = steps // nblk
    return pl.BlockSpec((rows // nblk, cols), lambda i: (jnp.maximum(i - first_step, 0) // per, 0))


def _cast_passengers(src_refs, dst_refs):
    for src, dst in zip(src_refs, dst_refs):
        dst[...] = src[...].astype(BF16)


def _ffn_kernel(x_ref, g_ref, wup_in, wdn_in, *rest, final_norm, n_cast, pre_steps):
    rest = list(rest)
    gf_ref = rest.pop(0) if final_norm else None
    cast_in, (o_ref, *cast_out) = rest[:n_cast], rest[n_cast:2 * n_cast + 1]
    scratch = rest[2 * n_cast + 1:]
    if pre_steps:
        xn_ref, act_ref, wup_ref, wdn_ref = scratch
        i = pl.program_id(0)

        @pl.when(i < pre_steps)
        def _():
            for src, dst in ((wup_in, wup_ref), (wdn_in, wdn_ref)):
                rows = src.shape[0]
                dst[pl.ds(pl.multiple_of(i * rows, BF16_TILE_ROWS), rows), :] = src[...].astype(BF16)

        @pl.when(i >= pre_steps)
        def _():
            _ffn_tile(x_ref, g_ref, wup_ref, wdn_ref, gf_ref, cast_in, o_ref, cast_out, xn_ref, act_ref)
    else:
        xn_ref, act_ref = scratch
        _ffn_tile(x_ref, g_ref, wup_in, wdn_in, gf_ref, cast_in, o_ref, cast_out, xn_ref, act_ref)


def _ffn_tile(x_ref, g_ref, wup_ref, wdn_ref, gf_ref, cast_in, o_ref, cast_out, xn_ref, act_ref):
    final_norm = gf_ref is not None
    tm = x_ref.shape[0]

    def swiglu_chunk(c, rows):
        cs = slice(c * FF_CHUNK, (c + 1) * FF_CHUNK)
        a = _dot(xn_ref[rows, :], wup_ref[:, cs])
        b = _dot(xn_ref[rows, :], wup_ref[:, D_FF + c * FF_CHUNK:D_FF + (c + 1) * FF_CHUNK])
        act_ref[rows, cs] = (a * jax.nn.sigmoid(a) * b).astype(BF16)

    for r in range(tm // ROW_GROUP):
        rows = slice(r * ROW_GROUP, (r + 1) * ROW_GROUP)
        xn_ref[rows, :] = _rmsnorm(x_ref[rows, :], g_ref[...]).astype(BF16)
        swiglu_chunk(0, rows)
    for c in range(1, N_FF_CHUNKS):
        swiglu_chunk(c, slice(None))
    _cast_passengers(cast_in, cast_out)
    for r in range(tm // ROW_GROUP):
        rows = slice(r * ROW_GROUP, (r + 1) * ROW_GROUP)
        y = x_ref[rows, :] + 0.5 * _dot(act_ref[rows, :], wdn_ref[...])
        if final_norm:
            y = _rmsnorm(y, gf_ref[...])
        o_ref[rows, :] = y


def _ffn(x, g, wup, wdn, gf, passengers, *, tm):
    n = x.shape[0]
    steps = n // tm
    final_norm = gf is not None
    pre = FFN_CAST_STEPS if wup.dtype == F32 else 0
    row_tile = lambda i: (jnp.maximum(i - pre, 0), 0)
    in_specs = [pl.BlockSpec((tm, D_MODEL), row_tile), _resident((1, D_MODEL))]
    scratch = [pltpu.VMEM((tm, D_MODEL), BF16), pltpu.VMEM((tm, D_FF), BF16)]
    if pre:
        cast_block = lambda i: (jnp.minimum(i, pre - 1), 0)
        in_specs += [pl.BlockSpec((D_MODEL // pre, 2 * D_FF), cast_block),
                     pl.BlockSpec((D_FF // pre, D_MODEL), cast_block)]
        scratch += [pltpu.VMEM((D_MODEL, 2 * D_FF), BF16), pltpu.VMEM((D_FF, D_MODEL), BF16)]
    else:
        in_specs += [_resident((D_MODEL, 2 * D_FF)), _resident((D_FF, D_MODEL))]
    args = [x, g, wup, wdn]
    if final_norm:
        in_specs.append(_resident((1, D_MODEL)))
        args.append(gf)
    cast_specs = [_passenger_spec(w.shape, steps, pre) for w in passengers]
    out = pl.pallas_call(
        functools.partial(_ffn_kernel, final_norm=final_norm, n_cast=len(passengers), pre_steps=pre),
        grid=(pre + steps,),
        in_specs=in_specs + cast_specs,
        out_specs=[pl.BlockSpec((tm, D_MODEL), row_tile)] + cast_specs,
        out_shape=[jax.ShapeDtypeStruct((n, D_MODEL), F32)]
        + [jax.ShapeDtypeStruct(w.shape, BF16) for w in passengers],
        scratch_shapes=scratch,
        compiler_params=pltpu.CompilerParams(
            dimension_semantics=("arbitrary",), vmem_limit_bytes=VMEM_LIMIT_BYTES),
        name="ffn_final" if final_norm else "ffn",
    )(*args, *passengers)
    return out[0], out[1:]


def _mixer_kernel(sinks_ref, h_ref, gm_ref, win_ref, wau_ref, wmix_ref, pscale_ref,
                  wpu_ref, wout_ref, bias_ref, o_ref,
                  u_scr, q_scr, kT_scr, vt_scr, vb_scr, z_scr, attn_scr, pool_scr, mrg_scr, gate_scr, *, tm):
    i = pl.program_id(0)
    tiles_per_seq = SEQ // tm
    blocks_per_tile = tm // BLOCK
    gate_row_groups = tm // GATE_ROWS
    assert gate_row_groups * (2 * D_MODEL // GATE_COLS) == blocks_per_tile * N_KV_HEADS
    tile_in_seq = i % tiles_per_seq
    seq_start = tile_in_seq == 0

    @pl.when(seq_start)
    def _():
        kT_scr[:, 0:BLOCK] = jnp.zeros((LANES, BLOCK), BF16)
        for scr in (vt_scr, vb_scr):
            scr[0:BLOCK, :] = jnp.zeros((BLOCK, scr.shape[1]), scr.dtype)
        z_scr[0:POOL_HIST, :] = jnp.zeros((POOL_HIST, POOL_WIDTH), F32)
        lane = lax.broadcasted_iota(jnp.int32, (tm + BLOCK, LANES), 1)
        for hk in range(N_KV_HEADS):
            c1 = hk * 2 * LANES + LANES
            vt_scr[:, c1:c1 + LANES] = jnp.where(lane < HEAD_DIM, 1.0, 0.0).astype(BF16)
            vb_scr[:, c1:c1 + LANES] = jnp.where(lane >= HEAD_DIM, 1.0, 0.0).astype(BF16)

    @pl.when(jnp.logical_not(seq_start))
    def _():
        kT_scr[:, 0:BLOCK] = kT_scr[:, tm:tm + BLOCK]
        for scr in (vt_scr, vb_scr):
            scr[0:BLOCK, :] = scr[tm:tm + BLOCK, :]
        z_scr[0:POOL_HIST, :] = z_scr[tm:tm + POOL_HIST, :]

    for r in range(tm // ROW_GROUP):
        rows = slice(r * ROW_GROUP, (r + 1) * ROW_GROUP)
        u_r = _rmsnorm(h_ref[rows, :], gm_ref[...]).astype(BF16)
        u_scr[rows, :] = u_r
        q_scr[rows, :] = (_dot(u_r, win_ref[:, 0:_K0]) * (HEAD_DIM ** -0.5 * LOG2E)).astype(BF16)
    u = u_scr[...]

    kv = _dot(u, win_ref[:, _K0:_Z0])
    kT_scr[:, BLOCK:] = kv[:, :LANES].T.astype(BF16)

    low = lax.broadcasted_iota(jnp.int32, (tm, LANES), 1) < HEAD_DIM
    v01 = kv[:, LANES:]
    v10 = pltpu.roll(v01, HEAD_DIM, 1)
    zero = jnp.zeros_like(v01)
    v_tops = (jnp.where(low, v01, zero), jnp.where(low, v10, zero))
    v_bots = (jnp.where(low, zero, v10), jnp.where(low, zero, v01))
    for hk in range(N_KV_HEADS):
        c0 = hk * 2 * LANES
        vt_scr[BLOCK:, c0:c0 + LANES] = v_tops[hk].astype(BF16)
        vb_scr[BLOCK:, c0:c0 + LANES] = v_bots[hk].astype(BF16)
    z_scr[POOL_HIST:, :] = _dot(u, win_ref[:, _Z0:_GA0])

    half_lane = lax.broadcasted_iota(jnp.int32, (BLOCK, LANES), 1) < HEAD_DIM
    kzero = jnp.zeros((HEAD_DIM, 2 * BLOCK), BF16)

    for b in range(blocks_per_tile):
        r0 = b * BLOCK
        first_block = seq_start.astype(jnp.int32) if b == 0 else 0
        for hk in range(N_KV_HEADS):
            kTb = kT_scr[hk * HEAD_DIM:(hk + 1) * HEAD_DIM, r0:r0 + 2 * BLOCK]
            k_even_odd = jnp.concatenate(
                [jnp.concatenate([kTb, kzero], axis=1), jnp.concatenate([kzero, kTb], axis=1)], axis=0)
            vt = vt_scr[r0:r0 + 2 * BLOCK, hk * 2 * LANES:(hk + 1) * 2 * LANES]
            vb = vb_scr[r0:r0 + 2 * BLOCK, hk * 2 * LANES:(hk + 1) * 2 * LANES]
            q0 = hk * Q_PER_KV * HEAD_DIM
            qs = jnp.concatenate(
                [q_scr[r0:r0 + BLOCK, q0 + p * LANES:q0 + (p + 1) * LANES] for p in range(PAIRS_PER_KV)],
                axis=0)
            s = _dot(qs, k_even_odd) + bias_ref[first_block, hk]
            s_even = s[:, :2 * BLOCK]
            s_odd = s[:, 2 * BLOCK:]
            step = b * N_KV_HEADS + hk
            grow = slice((step % gate_row_groups) * GATE_ROWS, (step % gate_row_groups + 1) * GATE_ROWS)
            gcol = (step // gate_row_groups) * GATE_COLS
            gate_scr[grow, gcol:gcol + GATE_COLS] = _dot(
                u_scr[grow, :], win_ref[:, _GA0 + gcol:_GA0 + gcol + GATE_COLS])
            e_even, e_odd, sink_terms = [], [], []
            for p in range(PAIRS_PER_KV):
                rows = slice(p * BLOCK, (p + 1) * BLOCK)
                head0 = hk * Q_PER_KV + 2 * p
                sink0 = sinks_ref[head0] * LOG2E
                sink1 = sinks_ref[head0 + 1] * LOG2E
                s0 = s_even[rows]
                s1 = s_odd[rows]
                m0 = jnp.maximum(jnp.max(s0, axis=-1, keepdims=True), sink0)
                m1 = jnp.maximum(jnp.max(s1, axis=-1, keepdims=True), sink1)
                e_even.append(jnp.exp2(s0 - m0).astype(BF16))
                e_odd.append(jnp.exp2(s1 - m1).astype(BF16))
                sink_terms.append(jnp.exp2(jnp.where(half_lane, sink0 - m0, sink1 - m1)))
            pv = (_dot(jnp.concatenate(e_even, axis=0), vt)
                  + _dot(jnp.concatenate(e_odd, axis=0), vb))
            for p in range(PAIRS_PER_KV):
                rows = slice(p * BLOCK, (p + 1) * BLOCK)
                out = pv[rows, :LANES] / (pv[rows, LANES:] + sink_terms[p])
                attn_scr[r0:r0 + BLOCK, q0 + p * LANES:q0 + (p + 1) * LANES] = out.astype(BF16)

    t = tile_in_seq * tm + lax.broadcasted_iota(jnp.int32, (tm, POOL_GROUP), 0)
    for gi, w in enumerate(POOL_WINDOWS):
        cs = slice(gi * POOL_GROUP, (gi + 1) * POOL_GROUP)
        acc = z_scr[:, cs]
        span = 1
        while span < w:
            acc = acc + pltpu.roll(acc, span, 0)
            span *= 2
        cnt = jnp.minimum(t + 1, w).astype(F32)
        pool_scr[:, cs] = (acc[POOL_HIST:] / cnt - z_scr[POOL_HIST:, cs]).astype(BF16)
    mixed = (_dot(pool_scr[...], wmix_ref[...]) * pscale_ref[...]).astype(BF16)

    half = D_MODEL // 2
    for c in range(2):
        cs = slice(c * half, (c + 1) * half)
        a = _dot(attn_scr[...], wau_ref[:, cs])
        p = _dot(mixed, wpu_ref[:, cs])
        ga = jax.nn.sigmoid(gate_scr[:, c * half:(c + 1) * half])
        gp = jax.nn.sigmoid(gate_scr[:, D_MODEL + c * half:D_MODEL + (c + 1) * half])
        mrg_scr[:, cs] = (ga * a + gp * p).astype(BF16)
    for r in range(tm // ROW_GROUP):
        rows = slice(r * ROW_GROUP, (r + 1) * ROW_GROUP)
        o_ref[rows, :] = h_ref[rows, :] + _dot(mrg_scr[rows, :], wout_ref[...])


def _attention_bias():
    hidx = np.arange(1, N_Q_HEADS + 1, dtype=np.float32)
    slopes = (2.0 ** (-8.0 * hidx / N_Q_HEADS)).astype(np.float32)
    qi = np.arange(BLOCK)[:, None] + BLOCK
    kj = np.arange(2 * BLOCK)[None, :]
    dist = qi - kj
    valid = (dist >= 0) & (dist < WINDOW)
    valid_first = valid & (kj >= BLOCK)
    lin = -(slopes[:, None, None] * dist.astype(np.float32)[None]).astype(np.float64) * LOG2E
    tables = []
    for vis in (valid, valid_first):
        per_head = np.where(vis[None], lin, -np.inf)
        per_head = per_head.reshape(N_KV_HEADS, PAIRS_PER_KV, 2, BLOCK, 2 * BLOCK)
        stacked = per_head.transpose(0, 1, 3, 2, 4).reshape(N_KV_HEADS, PAIRS_PER_KV * BLOCK, 4 * BLOCK)
        tables.append(stacked)
    return jnp.asarray(np.stack(tables).astype(np.float32))


def _mixer(h, sinks, gm, win, wau, wmix, pscale, wpu, wout, bias, *, tm):
    n = h.shape[0]
    kvw = N_KV_HEADS * LANES
    return pl.pallas_call(
        functools.partial(_mixer_kernel, tm=tm),
        grid=(n // tm,),
        in_specs=[
            pl.BlockSpec(memory_space=pltpu.SMEM),
            pl.BlockSpec((tm, D_MODEL), lambda i: (i, 0)),
            _resident((1, D_MODEL)),
            _resident((D_MODEL, IN_WIDTH)),
            _resident((ATTN_WIDTH, D_MODEL)),
            _resident((POOL_WIDTH, POOL_WIDTH)),
            _resident((1, POOL_WIDTH)),
            _resident((POOL_WIDTH, D_MODEL)),
            _resident((D_MODEL, D_MODEL)),
            _resident((2, N_KV_HEADS, PAIRS_PER_KV * BLOCK, 4 * BLOCK)),
        ],
        out_specs=pl.BlockSpec((tm, D_MODEL), lambda i: (i, 0)),
        out_shape=jax.ShapeDtypeStruct((n, D_MODEL), F32),
        scratch_shapes=[
            pltpu.VMEM((tm, D_MODEL), BF16),
            pltpu.VMEM((tm, ATTN_WIDTH), BF16),
            pltpu.VMEM((LANES, tm + BLOCK), BF16),
            pltpu.VMEM((tm + BLOCK, 2 * kvw), BF16),
            pltpu.VMEM((tm + BLOCK, 2 * kvw), BF16),
            pltpu.VMEM((tm + POOL_HIST, POOL_WIDTH), F32),
            pltpu.VMEM((tm, ATTN_WIDTH), BF16),
            pltpu.VMEM((tm, POOL_WIDTH), BF16),
            pltpu.VMEM((tm, D_MODEL), BF16),
            pltpu.VMEM((tm, 2 * D_MODEL), F32),
        ],
        compiler_params=pltpu.CompilerParams(
            dimension_semantics=("arbitrary",), vmem_limit_bytes=VMEM_LIMIT_BYTES),
        name="mixer",
    )(sinks, h, gm, win, wau, wmix, pscale, wpu, wout, bias)


def _block_diagonal(blocks):
    g, n, _ = blocks.shape
    out = jnp.zeros((g * n, g * n), blocks.dtype)
    for gi in range(g):
        out = out.at[gi * n:(gi + 1) * n, gi * n:(gi + 1) * n].set(blocks[gi])
    return out


def kernel(x, ffn1_norm, ffn1_w_up, ffn1_w_down, mix_norm, w_in, sinks, w_attn_up, pool_w_mix, pool_scale,
           w_pool_up, w_out, ffn2_norm, ffn2_w_up, ffn2_w_down, final_norm):
    batch, seq, d = x.shape
    depth = ffn1_norm.shape[0]
    assert (seq, d) == (SEQ, D_MODEL)
    h = x.reshape(batch * seq, d)
    bias = _attention_bias()
    ffn1_w = (ffn1_w_up[0], ffn1_w_down[0])
    for l in range(depth):
        last = l == depth - 1
        h, (win, wau, wmix, wpu, wout, *ffn2_w) = _ffn(
            h, ffn1_norm[l].reshape(1, d), *ffn1_w, None,
            [w_in[l], w_attn_up[l], _block_diagonal(pool_w_mix[l]), w_pool_up[l], w_out[l],
             ffn2_w_up[l], ffn2_w_down[l]], tm=FFN_ROWS)
        h = _mixer(h, sinks[l], mix_norm[l].reshape(1, d), win, wau, wmix,
                   pool_scale[l].reshape(1, POOL_WIDTH), wpu, wout, bias, tm=MIXER_ROWS)
        h, ffn1_w = _ffn(h, ffn2_norm[l].reshape(1, d), *ffn2_w, final_norm.reshape(1, d) if last else None,
                         [] if last else [ffn1_w_up[l + 1], ffn1_w_down[l + 1]], tm=FFN_ROWS)
    return h.reshape(batch, seq, d)
```
